```python
import math
import jax, jax.numpy as jnp
from jax import lax
import numpy as np

D_MODEL = 2048
BATCH = 4
SEQ = 4096
DEPTH = 1

HEAD_DIM = 128
N_HEADS = D_MODEL // HEAD_DIM
N_HEADS_A = N_HEADS // 2
N_HEADS_B = N_HEADS - N_HEADS_A
WIDTH_A = N_HEADS_A * HEAD_DIM
WIDTH_B = N_HEADS_B * HEAD_DIM

DIL_CONFIGS = ((128, 1), (512, 4), (2048, 16))
DIL_BLOCK = 128

NSA_KV_GROUPS = 2
NSA_HPG = N_HEADS_B // NSA_KV_GROUPS
KV_WIDTH_B = NSA_KV_GROUPS * HEAD_DIM
CMP_BLOCK = 32
CMP_STRIDE = 16
CMP_HIDDEN = 2 * HEAD_DIM
SEL_BLOCK = 64
SEL_TOP_N = 16
SEL_FORCED_LOCAL = 2
SEL_Q_CHUNK = 64
FORCE_BONUS = 1.0e3
WIN_SIZE = 512
WIN_BLOCK = 128
N_GATES = 3

FFN_HIDDEN = -(-(8 * D_MODEL) // (3 * 256)) * 256
RMS_EPS = 1e-6
NEG_INF = -1e30

COL_SIZES = (WIDTH_A, WIDTH_A, WIDTH_A,
             WIDTH_B,
             KV_WIDTH_B, KV_WIDTH_B,
             KV_WIDTH_B, KV_WIDTH_B,
             KV_WIDTH_B, KV_WIDTH_B,
             N_HEADS_B * N_GATES)
P_IN = sum(COL_SIZES)

kernel_name = "hybrid_dilated_nsa_swiglu_block"


def rmsnorm(x, g):
    xf = x.astype(jnp.float32)
    y = xf * lax.rsqrt(jnp.mean(xf * xf, axis=-1, keepdims=True) + RMS_EPS)
    return (y * g.astype(jnp.float32)).astype(x.dtype)


def alibi_slopes(n):
    return jnp.exp2(-8.0 * jnp.arange(1, n + 1, dtype=jnp.float32) / n)


def masked_softmax_f32(s, mask):
    s = jnp.where(mask, s, NEG_INF)
    m = jnp.max(s, axis=-1, keepdims=True)
    e = jnp.where(mask, jnp.exp(s - m), 0.0)
    return e / jnp.maximum(jnp.sum(e, axis=-1, keepdims=True), 1e-30)


def split_heads(a, n):
    b, s, _ = a.shape
    return a.reshape(b, s, n, HEAD_DIM).transpose(0, 2, 1, 3)


def dilated_attention_config(q, k, v, slopes, window, dilation):
    B, H, S, hd = q.shape
    L = S // dilation
    nb = -(-L // DIL_BLOCK)
    Lp = nb * DIL_BLOCK
    W = window // dilation
    n_prev = -(-W // DIL_BLOCK)
    scale = 1.0 / math.sqrt(hd)

    def to_sub(a):
        a = a.reshape(B, H, L, dilation, hd).transpose(0, 1, 3, 2, 4)
        a = jnp.pad(a, ((0, 0), (0, 0), (0, 0), (0, Lp - L), (0, 0)))
        return a.reshape(B, H, dilation, nb, DIL_BLOCK, hd)

    def band(a):
        a = jnp.pad(a, ((0, 0), (0, 0), (0, 0), (n_prev, 0), (0, 0), (0, 0)))
        return jnp.concatenate([a[:, :, :, i:i + nb] for i in range(n_prev + 1)], axis=4)

    qs = to_sub(q)
    kb, vb = band(to_sub(k)), band(to_sub(v))
    qi = jnp.arange(nb)[:, None] * DIL_BLOCK + jnp.arange(DIL_BLOCK)[None, :]
    kj = (jnp.arange(nb)[:, None] - n_prev) * DIL_BLOCK + jnp.arange((n_prev + 1) * DIL_BLOCK)[None, :]
    rel = qi[:, :, None] - kj[:, None, :]
    mask = (rel >= 0) & (rel <= W) & (kj[:, None, :] >= 0)
    s = jnp.einsum('bhrnqd,bhrnkd->bhrnqk', qs, kb).astype(jnp.float32) * scale
    s = s - slopes[:, None, None, None, None] * (rel * dilation).astype(jnp.float32)
    s = jnp.where(mask, s, NEG_INF)
    lse = jax.nn.logsumexp(s, axis=-1)
    p = jnp.exp(s - lse[..., None])
    o = jnp.einsum('bhrnqk,bhrnkd->bhrnqd', p.astype(v.dtype), vb)
    o = o.reshape(B, H, dilation, Lp, hd)[:, :, :, :L].transpose(0, 1, 3, 2, 4).reshape(B, H, S, hd)
    lse = lse.reshape(B, H, dilation, Lp)[:, :, :, :L].transpose(0, 1, 3, 2).reshape(B, H, S)
    return o, lse


def dilated_mixture(q, k, v, slopes):
    outs, lses = [], []
    for window, dilation in DIL_CONFIGS:
        o, l = dilated_attention_config(q, k, v, slopes, window, dilation)
        outs.append(o)
        lses.append(l)
    w = jax.nn.softmax(jnp.stack(lses, axis=0), axis=0)
    out = sum(w[i][..., None].astype(q.dtype) * outs[i] for i in range(len(outs)))
    return out


def compress_blocks(kv, pe, w1, w2):
    B, G, S, hd = kv.shape
    nc = S // CMP_STRIDE
    r = CMP_BLOCK // CMP_STRIDE
    n_cmp = nc - r + 1
    chunks = kv.reshape(B, G, nc, CMP_STRIDE, hd)
    blocks = jnp.concatenate([chunks[:, :, i:i + n_cmp] for i in range(r)], axis=3)
    blocks = blocks + pe
    flat = blocks.reshape(B, G, n_cmp, CMP_BLOCK * hd)
    return jax.nn.silu(flat @ w1) @ w2


def nsa_compressed(q, kc, vc, slopes):
    S = q.shape[3]
    n_cmp = kc.shape[2]
    scale = 1.0 / math.sqrt(q.shape[-1])
    t = jnp.arange(S)
    end = jnp.arange(n_cmp) * CMP_STRIDE + CMP_BLOCK - 1
    rel = t[:, None] - end[None, :]
    s = jnp.einsum('bghtd,bgcd->bghtc', q, kc).astype(jnp.float32) * scale
    s = s - slopes[:, :, None, None] * rel.astype(jnp.float32)
    p = masked_softmax_f32(s, rel >= 0)
    o = jnp.einsum('bghtc,bgcd->bghtd', p.astype(vc.dtype), vc)
    return o, p


def nsa_selected(q, k, v, p_cmp, slopes):
    B, G, Hq, S, hd = q.shape
    n_cmp = p_cmp.shape[-1]
    n_sel = S // SEL_BLOCK
    top_n = min(SEL_TOP_N, n_sel)
    scale = 1.0 / math.sqrt(hd)
    cstart = jnp.arange(n_cmp)[:, None] * CMP_STRIDE
    sstart = jnp.arange(n_sel)[None, :] * SEL_BLOCK
    overlap = ((cstart < sstart + SEL_BLOCK) & (cstart + CMP_BLOCK > sstart)).astype(jnp.float32)
    imp = jnp.einsum('bghtc,cn->bgtn', p_cmp, overlap)
    t = jnp.arange(S)
    blk = jnp.arange(n_sel)[None, :]
    cur = (t // SEL_BLOCK)[:, None]
    valid = blk <= cur
    forced = (blk == 0) | ((blk <= cur) & (blk > cur - SEL_FORCED_LOCAL))
    score = jnp.where(valid, imp + jnp.where(forced, FORCE_BONUS, 0.0), -1.0)
    top_s, top_i = lax.top_k(score, top_n)
    sel_ok = top_s >= 0.0

    nq = S // SEL_Q_CHUNK
    qc = q.reshape(B, G, Hq, nq, SEL_Q_CHUNK, hd).transpose(3, 0, 1, 2, 4, 5)
    ic = top_i.reshape(B, G, nq, SEL_Q_CHUNK, top_n).transpose(2, 0, 1, 3, 4)
    oc = sel_ok.reshape(B, G, nq, SEL_Q_CHUNK, top_n).transpose(2, 0, 1, 3, 4)
    tc = t.reshape(nq, SEL_Q_CHUNK)
    kblocks = k.reshape(B, G, n_sel, SEL_BLOCK, hd)
    vblocks = v.reshape(B, G, n_sel, SEL_BLOCK, hd)
    bi = jnp.arange(B)[:, None, None, None]
    gi = jnp.arange(G)[None, :, None, None]
    offs = jnp.arange(SEL_BLOCK)

    def one_chunk(args):
        q_, i_, ok_, t_ = args
        kg = kblocks[bi, gi, i_]
        vg = vblocks[bi, gi, i_]
        pos = i_[..., None] * SEL_BLOCK + offs
        rel = t_[None, None, :, None, None] - pos
        mask = ok_[..., None] & (rel >= 0)
        s = jnp.einsum('bghqd,bgqnkd->bghqnk', q_, kg).astype(jnp.float32) * scale
        s = s - slopes[:, :, None, None, None] * rel[:, :, None].astype(jnp.float32)
        s = jnp.where(mask[:, :, None], s, NEG_INF)
        s = s.reshape(B, G, Hq, SEL_Q_CHUNK, top_n * SEL_BLOCK)
        p = jax.nn.softmax(s, axis=-1)
        return jnp.einsum('bghqm,bgqmd->bghqd', p.astype(v.dtype),
                          vg.reshape(B, G, SEL_Q_CHUNK, top_n * SEL_BLOCK, hd))

    o = lax.map(one_chunk, (qc, ic, oc, tc))
    return o.transpose(1, 2, 3, 0, 4, 5).reshape(B, G, Hq, S, hd)


def nsa_window(q, k, v, slopes):
    B, G, Hq, S, hd = q.shape
    nb = S // WIN_BLOCK
    n_prev = -(-(WIN_SIZE - 1) // WIN_BLOCK)
    scale = 1.0 / math.sqrt(hd)
    qb = q.reshape(B, G, Hq, nb, WIN_BLOCK, hd)

    def band(a):
        a = a.reshape(B, G, nb, WIN_BLOCK, hd)
        a = jnp.pad(a, ((0, 0), (0, 0), (n_prev, 0), (0, 0), (0, 0)))
        return jnp.concatenate([a[:, :, i:i + nb] for i in range(n_prev + 1)], axis=3)

    kb, vb = band(k), band(v)
    qi = jnp.arange(nb)[:, None] * WIN_BLOCK + jnp.arange(WIN_BLOCK)[None, :]
    kj = (jnp.arange(nb)[:, None] - n_prev) * WIN_BLOCK + jnp.arange((n_prev + 1) * WIN_BLOCK)[None, :]
    rel = qi[:, :, None] - kj[:, None, :]
    mask = (rel >= 0) & (rel < WIN_SIZE) & (kj[:, None, :] >= 0)
    s = jnp.einsum('bghnqd,bgnkd->bghnqk', qb, kb).astype(jnp.float32) * scale
    s = s - slopes[:, :, None, None, None] * rel.astype(jnp.float32)
    p = masked_softmax_f32(s, mask)
    o = jnp.einsum('bghnqk,bgnkd->bghnqd', p.astype(v.dtype), vb)
    return o.reshape(B, G, Hq, S, hd)


def setup_inputs(seed: int = 0) -> dict:
    key = jax.random.key(seed)
    ks = jax.random.split(key, 20)
    f32 = jnp.float32
    nrm = lambda k, shape, fan_in: jax.random.normal(k, shape, f32) * (fan_in ** -0.5)
    gain = lambda k, shape: 1.0 + 0.02 * jax.random.normal(k, shape, f32)
    return {
        "x": jax.random.normal(ks[0], (BATCH, SEQ, D_MODEL), f32),
        "norm1_g": gain(ks[1], (DEPTH, D_MODEL)),
        "w_in": nrm(ks[2], (DEPTH, D_MODEL, P_IN), D_MODEL),
        "cmp_pe_k": 0.1 * jax.random.normal(ks[3], (DEPTH, CMP_BLOCK, HEAD_DIM), f32),
        "cmp_w1_k": nrm(ks[4], (DEPTH, CMP_BLOCK * HEAD_DIM, CMP_HIDDEN), CMP_BLOCK * HEAD_DIM),
        "cmp_w2_k": nrm(ks[5], (DEPTH, CMP_HIDDEN, HEAD_DIM), CMP_HIDDEN),
        "cmp_pe_v": 0.1 * jax.random.normal(ks[6], (DEPTH, CMP_BLOCK, HEAD_DIM), f32),
        "cmp_w1_v": nrm(ks[7], (DEPTH, CMP_BLOCK * HEAD_DIM, CMP_HIDDEN), CMP_BLOCK * HEAD_DIM),
        "cmp_w2_v": nrm(ks[8], (DEPTH, CMP_HIDDEN, HEAD_DIM), CMP_HIDDEN),
        "grp_norm_a": gain(ks[9], (DEPTH, WIDTH_A)),
        "grp_norm_b": gain(ks[10], (DEPTH, WIDTH_B)),
        "w_out": nrm(ks[11], (DEPTH, WIDTH_A + WIDTH_B, D_MODEL), WIDTH_A + WIDTH_B),
        "norm2_g": gain(ks[12], (DEPTH, D_MODEL)),
        "w_gate": nrm(ks[13], (DEPTH, D_MODEL, FFN_HIDDEN), D_MODEL),
        "w_up": nrm(ks[14], (DEPTH, D_MODEL, FFN_HIDDEN), D_MODEL),
        "w_down": nrm(ks[15], (DEPTH, FFN_HIDDEN, D_MODEL), FFN_HIDDEN),
        "final_g": gain(ks[16], (D_MODEL,)),
    }


def reference(x, norm1_g, w_in, cmp_pe_k, cmp_w1_k, cmp_w2_k, cmp_pe_v, cmp_w1_v, cmp_w2_v,
              grp_norm_a, grp_norm_b, w_out, norm2_g, w_gate, w_up, w_down, final_g):
    B, S, _ = x.shape
    G, Hq = NSA_KV_GROUPS, NSA_HPG
    slopes = alibi_slopes(N_HEADS)
    slopes_a = slopes[0::2]
    slopes_b = slopes[1::2].reshape(G, Hq)
    offsets = np.cumsum(COL_SIZES)[:-1].tolist()
    h = x
    for l in range(DEPTH):
        hn = rmsnorm(h, norm1_g[l])
        proj = hn @ w_in[l]
        (q_a, k_a, v_a, q_b, k_c, v_c, k_s, v_s, k_w, v_w, g_b) = jnp.split(proj, offsets, axis=-1)

        o_a = dilated_mixture(split_heads(q_a, N_HEADS_A), split_heads(k_a, N_HEADS_A),
                              split_heads(v_a, N_HEADS_A), slopes_a)
        o_a = o_a.transpose(0, 2, 1, 3).reshape(B, S, WIDTH_A)

        qb = split_heads(q_b, N_HEADS_B).reshape(B, G, Hq, S, HEAD_DIM)
        kc_raw, vc_raw = split_heads(k_c, G), split_heads(v_c, G)
        kc = compress_blocks(kc_raw, cmp_pe_k[l], cmp_w1_k[l], cmp_w2_k[l])
        vc = compress_blocks(vc_raw, cmp_pe_v[l], cmp_w1_v[l], cmp_w2_v[l])
        o_cmp, p_cmp = nsa_compressed(qb, kc, vc, slopes_b)
        o_sel = nsa_selected(qb, split_heads(k_s, G), split_heads(v_s, G), p_cmp, slopes_b)
        o_win = nsa_window(qb, split_heads(k_w, G), split_heads(v_w, G), slopes_b)
        gates = jax.nn.sigmoid(g_b.reshape(B, S, N_HEADS_B, N_GATES)).transpose(0, 2, 1, 3)
        gates = gates.reshape(B, G, Hq, S, N_GATES)
        o_b = (gates[..., 0:1] * o_cmp + gates[..., 1:2] * o_sel + gates[..., 2:3] * o_win)
        o_b = o_b.reshape(B, N_HEADS_B, S, HEAD_DIM).transpose(0, 2, 1, 3).reshape(B, S, WIDTH_B)

        mixed = jnp.concatenate([rmsnorm(o_a, grp_norm_a[l]), rmsnorm(o_b, grp_norm_b[l])], axis=-1)
        h = h + mixed @ w_out[l]

        hn2 = rmsnorm(h, norm2_g[l])
        h = h + (jax.nn.silu(hn2 @ w_gate[l]) * (hn2 @ w_up[l])) @ w_down[l]
    return rmsnorm(h, final_g)
```

```python
import functools
import math

import jax
import jax.numpy as jnp
from jax import lax
from jax.experimental import pallas as pl
from jax.experimental.pallas import tpu as pltpu

F32 = jnp.float32
BF16 = jnp.bfloat16

D_MODEL = 2048
HEAD_DIM = 128
N_HEADS = 16
N_HEADS_A = 8
N_HEADS_B = 8
WIDTH_A = N_HEADS_A * HEAD_DIM
WIDTH_B = N_HEADS_B * HEAD_DIM
DIL_CONFIGS = ((128, 1), (512, 4), (2048, 16))
DIL_BLOCK = 128
NSA_KV_GROUPS = 2
NSA_HPG = N_HEADS_B // NSA_KV_GROUPS
KV_WIDTH_B = NSA_KV_GROUPS * HEAD_DIM
CMP_BLOCK = 32
CMP_STRIDE = 16
CMP_HIDDEN = 2 * HEAD_DIM
SEL_BLOCK = 64
SEL_TOP_N = 16
SEL_FORCED_LOCAL = 2
FORCE_BONUS = 1.0e3
WIN_SIZE = 512
N_GATES = 3
FFN_HIDDEN = 5632
RMS_EPS = 1e-6
NEG_INF = -1e30

COLS_A = 3 * WIDTH_A
COLS_B = WIDTH_B + 6 * KV_WIDTH_B
LANE = 128
GATE_PAD = NSA_KV_GROUPS * LANE

VMEM_LIMIT = 56 * 1024 * 1024


def _cparams(sem):
    return pltpu.CompilerParams(dimension_semantics=sem, vmem_limit_bytes=VMEM_LIMIT)


def _nt_dot(a, b):
    return lax.dot_general(a, b, (((1,), (1,)), ((), ())), preferred_element_type=F32)


IN_TM = 1024
IN_TN = 512
N_TILES_A = COLS_A // IN_TN
N_TILES_B = COLS_B // IN_TN
Q_B_TILE0 = N_TILES_A


def _inproj_kernel(x_ref, g_ref, w_ref, wg_ref, oa_ref, ob_ref, og_ref, xn_ref, *, scale):
    j = pl.program_id(1)

    @pl.when(j == 0)
    def _():
        x = x_ref[...]
        y = x * lax.rsqrt(jnp.mean(x * x, axis=-1, keepdims=True) + RMS_EPS)
        xn = (y * g_ref[...]).astype(BF16)
        xn_ref[...] = xn
        og_ref[...] = jnp.dot(xn, wg_ref[...], preferred_element_type=F32)

    acc = jnp.dot(xn_ref[...], w_ref[...], preferred_element_type=F32)
    is_q = (j < 2) | ((j >= Q_B_TILE0) & (j < Q_B_TILE0 + 2))
    acc = acc * jnp.where(is_q, scale, 1.0).astype(F32)

    @pl.when(j < N_TILES_A)
    def _():
        oa_ref[...] = acc.astype(BF16)

    @pl.when(j >= N_TILES_A)
    def _():
        ob_ref[...] = acc.astype(BF16)


def _in_projection(x2, g, w_main, w_gate):
    T = x2.shape[0]
    grid = (T // IN_TM, N_TILES_A + N_TILES_B)
    return pl.pallas_call(
        functools.partial(_inproj_kernel, scale=1.0 / math.sqrt(HEAD_DIM)),
        grid=grid,
        in_specs=[
            pl.BlockSpec((IN_TM, D_MODEL), lambda i, j: (i, 0)),
            pl.BlockSpec((1, D_MODEL), lambda i, j: (0, 0)),
            pl.BlockSpec((D_MODEL, IN_TN), lambda i, j: (0, j)),
            pl.BlockSpec((D_MODEL, GATE_PAD), lambda i, j: (0, 0)),
        ],
        out_specs=[
            pl.BlockSpec((IN_TM, IN_TN), lambda i, j: (i, jnp.minimum(j, N_TILES_A - 1))),
            pl.BlockSpec((IN_TM, IN_TN), lambda i, j: (i, jnp.maximum(j - N_TILES_A, 0))),
            pl.BlockSpec((IN_TM, GATE_PAD), lambda i, j: (i, 0)),
        ],
        out_shape=[
            jax.ShapeDtypeStruct((T, COLS_A), BF16),
            jax.ShapeDtypeStruct((T, COLS_B), BF16),
            jax.ShapeDtypeStruct((T, GATE_PAD), F32),
        ],
        scratch_shapes=[pltpu.VMEM((IN_TM, D_MODEL), BF16)],
        compiler_params=_cparams(("arbitrary", "arbitrary")),
        name="in_projection",
    )(x2, g, w_main, w_gate)


def _dilated_kernel(slopes_ref, q_ref, kc_ref, kp_ref, vc_ref, vp_ref, o_ref, lse_ref, *,
                    dilation, span):
    n = pl.program_id(2)
    row = lax.broadcasted_iota(jnp.int32, (DIL_BLOCK, DIL_BLOCK), 0)
    col = lax.broadcasted_iota(jnp.int32, (DIL_BLOCK, DIL_BLOCK), 1)
    rel_c = row - col
    rel_p = rel_c + DIL_BLOCK
    mask_c = rel_c >= 0
    mask_p = (rel_p <= span) & (n > 0)
    dist_c = (rel_c * dilation).astype(F32)
    dist_p = (rel_p * dilation).astype(F32)
    lane = lax.broadcasted_iota(jnp.int32, (DIL_BLOCK, LANE), 1)
    lse_tile = jnp.zeros((DIL_BLOCK, LANE), F32)
    for h in range(N_HEADS_A):
        hs = slice(h * HEAD_DIM, (h + 1) * HEAD_DIM)
        slope = slopes_ref[h]
        q = q_ref[0, :, hs]
        s_c = _nt_dot(q, kc_ref[0, :, hs])
        s_p = _nt_dot(q, kp_ref[0, :, hs])
        s_c = jnp.where(mask_c, s_c - slope * dist_c, NEG_INF)
        s_p = jnp.where(mask_p, s_p - slope * dist_p, NEG_INF)
        m = jnp.maximum(jnp.max(s_c, axis=-1, keepdims=True), jnp.max(s_p, axis=-1, keepdims=True))
        e_c = jnp.exp(s_c - m)
        e_p = jnp.exp(s_p - m)
        l = jnp.sum(e_c, axis=-1, keepdims=True) + jnp.sum(e_p, axis=-1, keepdims=True)
        acc = jnp.dot(e_c.astype(BF16), vc_ref[0, :, hs], preferred_element_type=F32)
        acc = acc + jnp.dot(e_p.astype(BF16), vp_ref[0, :, hs], preferred_element_type=F32)
        o_ref[0, :, hs] = acc / l
        lse_tile = jnp.where(lane == h, m + jnp.log(l), lse_tile)
    lse_ref[0] = lse_tile


def _dilated_attention(qkv_a, slopes_a, B, S, window, dilation):
    L = S // dilation
    nb = L // DIL_BLOCK
    a3 = qkv_a.reshape(B, L, dilation * COLS_A)
    blk = (1, DIL_BLOCK, WIDTH_A)
    prev = lambda n: jnp.maximum(n - 1, 0)
    o, lse = pl.pallas_call(
        functools.partial(_dilated_kernel, dilation=dilation, span=window // dilation),
        grid=(B, dilation, nb),
        in_specs=[
            pl.BlockSpec(memory_space=pltpu.SMEM),
            pl.BlockSpec(blk, lambda b, r, n: (b, n, 3 * r)),
            pl.BlockSpec(blk, lambda b, r, n: (b, n, 3 * r + 1)),
            pl.BlockSpec(blk, lambda b, r, n: (b, prev(n), 3 * r + 1)),
            pl.BlockSpec(blk, lambda b, r, n: (b, n, 3 * r + 2)),
            pl.BlockSpec(blk, lambda b, r, n: (b, prev(n), 3 * r + 2)),
        ],
        out_specs=[
            pl.BlockSpec(blk, lambda b, r, n: (b, n, r)),
            pl.BlockSpec((1, DIL_BLOCK, LANE), lambda b, r, n: (b, n, r)),
        ],
        out_shape=[
            jax.ShapeDtypeStruct((B, L, dilation * WIDTH_A), F32),
            jax.ShapeDtypeStruct((B, L, dilation * LANE), F32),
        ],
        compiler_params=_cparams(("arbitrary", "arbitrary", "arbitrary")),
        name=f"dilated_attention_d{dilation}",
    )(slopes_a, a3, a3, a3, a3, a3)
    return o.reshape(B * S, WIDTH_A), lse.reshape(B * S, LANE)


CHUNK_FLAT = CMP_STRIDE * HEAD_DIM


def _compress_kernel(x_ref, pe_ref, w1_ref, w2_ref, o_ref):
    x = x_ref[0, 0].astype(F32)
    first = jnp.dot((x + pe_ref[0:1, :]).astype(BF16), w1_ref[0:CHUNK_FLAT, :],
                    preferred_element_type=F32)
    second = jnp.dot((x + pe_ref[1:2, :]).astype(BF16), w1_ref[CHUNK_FLAT:2 * CHUNK_FLAT, :],
                     preferred_element_type=F32)
    nc = x.shape[0]
    hidden = first + pltpu.roll(second, shift=nc - 1, axis=0)
    act = hidden * jax.nn.sigmoid(hidden)
    o_ref[0, 0] = jnp.dot(act.astype(BF16), w2_ref[...], preferred_element_type=F32).astype(BF16)


def _compress(chunks, pe2, w1, w2):
    B, G, nc, _ = chunks.shape
    return pl.pallas_call(
        _compress_kernel,
        grid=(B, G),
        in_specs=[
            pl.BlockSpec((1, 1, nc, CHUNK_FLAT), lambda b, g: (b, g, 0, 0)),
            pl.BlockSpec((2, CHUNK_FLAT), lambda b, g: (0, 0)),
            pl.BlockSpec((2 * CHUNK_FLAT, CMP_HIDDEN), lambda b, g: (0, 0)),
            pl.BlockSpec((CMP_HIDDEN, HEAD_DIM), lambda b, g: (0, 0)),
        ],
        out_specs=pl.BlockSpec((1, 1, nc, HEAD_DIM), lambda b, g: (b, g, 0, 0)),
        out_shape=jax.ShapeDtypeStruct((B, G, nc, HEAD_DIM), BF16),
        compiler_params=_cparams(("arbitrary", "arbitrary")),
        name="nsa_compress",
    )(chunks, pe2, w1, w2)


NSA_TQ = 128
SEL_PER_TILE = NSA_TQ // SEL_BLOCK
WIN_TILES = -(-(WIN_SIZE - 1) // NSA_TQ)


def _stack_heads(a):
    return jnp.concatenate([a[:, h * HEAD_DIM:(h + 1) * HEAD_DIM] for h in range(NSA_HPG)], axis=0)


def _nsa_kernel(slopes_ref, q_ref, kc_ref, vc_ref, ks_ref, vs_ref, kw_ref, vw_ref, gate_ref,
                ovl_ref, o_ref, *, n_sel):
    g = pl.program_id(1)
    qi = pl.program_id(2)
    t0 = qi * NSA_TQ
    rows = NSA_HPG * NSA_TQ

    q = _stack_heads(q_ref[0])
    slope_col = jnp.concatenate(
        [jnp.full((NSA_TQ, 1), slopes_ref[g * NSA_HPG + h], F32) for h in range(NSA_HPG)], axis=0)
    t_col = t0 + (lax.broadcasted_iota(jnp.int32, (rows, 1), 0) & (NSA_TQ - 1))

    n_cmp_pad = kc_ref.shape[2]
    cend = (lax.broadcasted_iota(jnp.int32, (1, n_cmp_pad), 1) * CMP_STRIDE + (CMP_BLOCK - 1))
    rel_c = t_col - cend
    ok_c = rel_c >= 0
    s = _nt_dot(q, kc_ref[0, 0])
    s = s - slope_col * rel_c.astype(F32)
    s = jnp.where(ok_c, s, NEG_INF)
    m = jnp.max(s, axis=-1, keepdims=True)
    e = jnp.where(ok_c, jnp.exp(s - m), 0.0)
    p = e / jnp.maximum(jnp.sum(e, axis=-1, keepdims=True), 1e-30)
    o_cmp = jnp.dot(p.astype(BF16), vc_ref[0, 0], preferred_element_type=F32)

    p_grp = p[0:NSA_TQ]
    for h in range(1, NSA_HPG):
        p_grp = p_grp + p[h * NSA_TQ:(h + 1) * NSA_TQ]
    p_hi = p_grp.astype(BF16)
    p_lo = (p_grp - p_hi.astype(F32)).astype(BF16)
    ovl = ovl_ref[...]
    imp_t = _nt_dot(ovl, p_hi) + _nt_dot(ovl, p_lo)
    blk = lax.broadcasted_iota(jnp.int32, (n_sel, NSA_TQ), 0)
    cur = (t0 + lax.broadcasted_iota(jnp.int32, (n_sel, NSA_TQ), 1)) // SEL_BLOCK
    valid = blk <= cur
    forced = (blk == 0) | (valid & (blk > cur - SEL_FORCED_LOCAL))
    score = jnp.where(valid, imp_t + jnp.where(forced, FORCE_BONUS, 0.0), -1.0)
    rank = jnp.zeros((n_sel, NSA_TQ), jnp.int32)
    for mth in range(n_sel):
        other = score[mth:mth + 1, :]
        tie_first = jnp.where(blk > mth, 1, 0)
        rank = rank + jnp.where(other > score, 1, jnp.where(other == score, tie_first, 0))
    member_t = jnp.where(valid, jnp.where(rank < SEL_TOP_N, 1.0, 0.0), 0.0).astype(BF16)
    eye = jnp.where((lax.broadcasted_iota(jnp.int32, (rows, NSA_TQ), 0) & (NSA_TQ - 1))
                    == lax.broadcasted_iota(jnp.int32, (rows, NSA_TQ), 1), 1.0, 0.0).astype(BF16)
    member = _nt_dot(eye, member_t).astype(BF16)

    lane_pos = lax.broadcasted_iota(jnp.int32, (1, NSA_TQ), 1)
    sel_row = lax.broadcasted_iota(jnp.int32, (n_sel, NSA_TQ), 0)
    sel_lane_blk = lax.broadcasted_iota(jnp.int32, (n_sel, NSA_TQ), 1) // SEL_BLOCK

    def flash_step(k, v, bias, carry):
        m_i, l_i, acc = carry
        s_ = _nt_dot(q, k) + bias
        m_new = jnp.maximum(m_i, jnp.max(s_, axis=-1, keepdims=True))
        alpha = jnp.exp(m_i - m_new)
        p_ = jnp.exp(s_ - m_new)
        l_new = alpha * l_i + jnp.sum(p_, axis=-1, keepdims=True)
        acc_new = alpha * acc + jnp.dot(p_.astype(BF16), v, preferred_element_type=F32)
        return m_new, l_new, acc_new

    init = (jnp.full((rows, 1), NEG_INF, F32), jnp.zeros((rows, 1), F32),
            jnp.zeros((rows, HEAD_DIM), F32))

    def sel_body(j, carry):
        start = pl.multiple_of(j * NSA_TQ, NSA_TQ)
        pos = start + lane_pos
        expand = jnp.where(sel_row == j * SEL_PER_TILE + sel_lane_blk, 1.0, 0.0).astype(BF16)
        chosen = jnp.dot(member, expand, preferred_element_type=F32)
        allowed = jnp.where(t_col >= pos, chosen, 0.0)
        bias = jnp.where(allowed > 0.5, 0.0, NEG_INF) + slope_col * (pos - t0).astype(F32)
        return flash_step(ks_ref[0, pl.ds(start, NSA_TQ), :], vs_ref[0, pl.ds(start, NSA_TQ), :],
                          bias, carry)

    _, l_s, acc_s = lax.fori_loop(0, qi + 1, sel_body, init)
    o_sel = acc_s / l_s

    def win_body(i, carry):
        j = qi - i
        start = pl.multiple_of(j * NSA_TQ, NSA_TQ)
        pos = start + lane_pos
        rel = t_col - pos
        in_win = jnp.where(rel >= 0, jnp.where(rel < WIN_SIZE, 0.0, NEG_INF), NEG_INF)
        bias = in_win + slope_col * (pos - t0).astype(F32)
        return flash_step(kw_ref[0, pl.ds(start, NSA_TQ), :], vw_ref[0, pl.ds(start, NSA_TQ), :],
                          bias, carry)

    _, l_w, acc_w = lax.fori_loop(0, jnp.minimum(qi, WIN_TILES) + 1, win_body, init)
    o_win = acc_w / l_w

    gates = jax.nn.sigmoid(gate_ref[0])
    for h in range(NSA_HPG):
        rs = slice(h * NSA_TQ, (h + 1) * NSA_TQ)
        gsel = [gates[:, h * N_GATES + k:h * N_GATES + k + 1] for k in range(N_GATES)]
        o_ref[0, :, h * HEAD_DIM:(h + 1) * HEAD_DIM] = (
            gsel[0] * o_cmp[rs] + gsel[1] * o_sel[rs] + gsel[2] * o_win[rs])


def _nsa_attention(qkv_b, kc, vc, gates, slopes_b, overlap_t, B, S):
    b3 = qkv_b.reshape(B, S, COLS_B)
    g3 = gates.reshape(B, S, GATE_PAD)
    n_cmp_pad = kc.shape[2]
    n_sel = S // SEL_BLOCK
    q_w = NSA_HPG * HEAD_DIM
    kv0 = WIDTH_B // HEAD_DIM
    per = KV_WIDTH_B // HEAD_DIM
    seq_blk = (1, S, HEAD_DIM)
    cmp_blk = (1, 1, n_cmp_pad, HEAD_DIM)
    out = pl.pallas_call(
        functools.partial(_nsa_kernel, n_sel=n_sel),
        grid=(B, NSA_KV_GROUPS, S // NSA_TQ),
        in_specs=[
            pl.BlockSpec(memory_space=pltpu.SMEM),
            pl.BlockSpec((1, NSA_TQ, q_w), lambda b, g, i: (b, i, g)),
            pl.BlockSpec(cmp_blk, lambda b, g, i: (b, g, 0, 0)),
            pl.BlockSpec(cmp_blk, lambda b, g, i: (b, g, 0, 0)),
            pl.BlockSpec(seq_blk, lambda b, g, i: (b, 0, kv0 + 2 * per + g)),
            pl.BlockSpec(seq_blk, lambda b, g, i: (b, 0, kv0 + 3 * per + g)),
            pl.BlockSpec(seq_blk, lambda b, g, i: (b, 0, kv0 + 4 * per + g)),
            pl.BlockSpec(seq_blk, lambda b, g, i: (b, 0, kv0 + 5 * per + g)),
            pl.BlockSpec((1, NSA_TQ, LANE), lambda b, g, i: (b, i, g)),
            pl.BlockSpec((n_sel, n_cmp_pad), lambda b, g, i: (0, 0)),
        ],
        out_specs=pl.BlockSpec((1, NSA_TQ, q_w), lambda b, g, i: (b, i, g)),
        out_shape=jax.ShapeDtypeStruct((B, S, WIDTH_B), F32),
        compiler_params=_cparams(("arbitrary", "arbitrary", "arbitrary")),
        name="nsa_attention",
    )(slopes_b, b3, kc, vc, b3, b3, b3, b3, g3, overlap_t)
    return out.reshape(B * S, WIDTH_B)


OUT_TM = 256


def _rms(x, g):
    return x * lax.rsqrt(jnp.mean(x * x, axis=-1, keepdims=True) + RMS_EPS) * g


def _outproj_kernel(o1_ref, o2_ref, o3_ref, l1_ref, l2_ref, l3_ref, ob_ref, x_ref, ga_ref, gb_ref,
                    w_ref, g2_ref, h_ref, hn_ref):
    l1, l2, l3 = l1_ref[...], l2_ref[...], l3_ref[...]
    mx = jnp.maximum(jnp.maximum(l1, l2), l3)
    e1, e2, e3 = jnp.exp(l1 - mx), jnp.exp(l2 - mx), jnp.exp(l3 - mx)
    den = e1 + e2 + e3
    w1, w2, w3 = e1 / den, e2 / den, e3 / den
    parts = []
    for h in range(N_HEADS_A):
        hs = slice(h * HEAD_DIM, (h + 1) * HEAD_DIM)
        parts.append(w1[:, h:h + 1] * o1_ref[:, hs] + w2[:, h:h + 1] * o2_ref[:, hs]
                     + w3[:, h:h + 1] * o3_ref[:, hs])
    o_a = jnp.concatenate(parts, axis=-1)
    mixed = jnp.concatenate([_rms(o_a, ga_ref[...]), _rms(ob_ref[...], gb_ref[...])], axis=-1)
    h = x_ref[...] + jnp.dot(mixed.astype(BF16), w_ref[...], preferred_element_type=F32)
    h_ref[...] = h
    hn_ref[...] = _rms(h, g2_ref[...]).astype(BF16)


def _out_projection(o_as, lses, o_b, x2, ga, gb, w_out, g2):
    T = x2.shape[0]
    row = lambda w: pl.BlockSpec((OUT_TM, w), lambda i: (i, 0))
    full = lambda r, c: pl.BlockSpec((r, c), lambda i: (0, 0))
    return pl.pallas_call(
        _outproj_kernel,
        grid=(T // OUT_TM,),
        in_specs=[row(WIDTH_A)] * 3 + [row(LANE)] * 3 + [row(WIDTH_B), row(D_MODEL),
                  full(1, WIDTH_A), full(1, WIDTH_B), full(WIDTH_A + WIDTH_B, D_MODEL), full(1, D_MODEL)],
        out_specs=[row(D_MODEL), row(D_MODEL)],
        out_shape=[jax.ShapeDtypeStruct((T, D_MODEL), F32), jax.ShapeDtypeStruct((T, D_MODEL), BF16)],
        compiler_params=_cparams(("arbitrary",)),
        name="out_projection",
    )(*o_as, *lses, o_b, x2, ga, gb, w_out, g2)


FFN_TM = 512
FFN_TH = 512


def _ffn_kernel(hn_ref, h_ref, wg_ref, wu_ref, wd_ref, gf_ref, o_ref, acc_ref):
    j = pl.program_id(1)
    hn = hn_ref[...]
    gate = jnp.dot(hn, wg_ref[...], preferred_element_type=F32)
    up = jnp.dot(hn, wu_ref[...], preferred_element_type=F32)
    act = (gate * jax.nn.sigmoid(gate) * up).astype(BF16)
    part = jnp.dot(act, wd_ref[...], preferred_element_type=F32)

    @pl.when(j == 0)
    def _():
        acc_ref[...] = part

    @pl.when(j > 0)
    def _():
        acc_ref[...] += part

    @pl.when(j == pl.num_programs(1) - 1)
    def _():
        o_ref[...] = _rms(h_ref[...] + acc_ref[...], gf_ref[...])


def _ffn(hn2, h, w_gate, w_up, w_down, gf):
    T = h.shape[0]
    return pl.pallas_call(
        _ffn_kernel,
        grid=(T // FFN_TM, FFN_HIDDEN // FFN_TH),
        in_specs=[
            pl.BlockSpec((FFN_TM, D_MODEL), lambda i, j: (i, 0)),
            pl.BlockSpec((FFN_TM, D_MODEL), lambda i, j: (i, 0)),
            pl.BlockSpec((D_MODEL, FFN_TH), lambda i, j: (0, j)),
            pl.BlockSpec((D_MODEL, FFN_TH), lambda i, j: (0, j)),
            pl.BlockSpec((FFN_TH, D_MODEL), lambda i, j: (j, 0)),
            pl.BlockSpec((1, D_MODEL), lambda i, j: (0, 0)),
        ],
        out_specs=pl.BlockSpec((FFN_TM, D_MODEL), lambda i, j: (i, 0)),
        out_shape=jax.ShapeDtypeStruct((T, D_MODEL), F32),
        scratch_shapes=[pltpu.VMEM((FFN_TM, D_MODEL), F32)],
        compiler_params=_cparams(("arbitrary", "arbitrary")),
        name="swiglu_ffn",
    )(hn2, h, w_gate, w_up, w_down, gf)


def _chunk_view(qkv_b3, col0):
    B, S, _ = qkv_b3.shape
    a = qkv_b3[:, :, col0:col0 + KV_WIDTH_B]
    a = a.reshape(B, S // CMP_STRIDE, CMP_STRIDE, NSA_KV_GROUPS, HEAD_DIM)
    return a.transpose(0, 3, 1, 2, 4).reshape(B, NSA_KV_GROUPS, S // CMP_STRIDE, CHUNK_FLAT)


def _overlap_t(n_sel, n_cmp_pad):
    cstart = jnp.arange(n_cmp_pad)[None, :] * CMP_STRIDE
    sstart = jnp.arange(n_sel)[:, None] * SEL_BLOCK
    return ((cstart < sstart + SEL_BLOCK) & (cstart + CMP_BLOCK > sstart)).astype(BF16)


def kernel(x, norm1_g, w_in, cmp_pe_k, cmp_w1_k, cmp_w2_k, cmp_pe_v, cmp_w1_v, cmp_w2_v, grp_norm_a,
           grp_norm_b, w_out, norm2_g, w_gate, w_up, w_down, final_g):
    B, S, D = x.shape
    assert D == D_MODEL and S % (DIL_BLOCK * max(d for _, d in DIL_CONFIGS)) == 0
    assert (B * S) % IN_TM == 0 and w_in.shape[0] == 1, "single-layer block only"
    T = B * S
    slopes = jnp.exp2(-8.0 * jnp.arange(1, N_HEADS + 1, dtype=F32) / N_HEADS)
    slopes_a, slopes_b = slopes[0::2], slopes[1::2]
    n_main = COLS_A + COLS_B
    per_group = NSA_HPG * N_GATES

    x2 = x.reshape(T, D)
    w_main = w_in[0][:, :n_main].astype(BF16)
    w_g = w_in[0][:, n_main:].reshape(D, NSA_KV_GROUPS, per_group)
    w_g = jnp.pad(w_g, ((0, 0), (0, 0), (0, LANE - per_group))).reshape(D, GATE_PAD).astype(BF16)
    qkv_a, qkv_b, gates = _in_projection(x2, norm1_g[0][None, :], w_main, w_g)

    dil = [_dilated_attention(qkv_a, slopes_a, B, S, w, d) for w, d in DIL_CONFIGS]

    b3 = qkv_b.reshape(B, S, COLS_B)
    pe2 = lambda pe: pe.reshape(2, CHUNK_FLAT)
    kc = _compress(_chunk_view(b3, WIDTH_B), pe2(cmp_pe_k[0]), cmp_w1_k[0].astype(BF16),
                   cmp_w2_k[0].astype(BF16))
    vc = _compress(_chunk_view(b3, WIDTH_B + KV_WIDTH_B), pe2(cmp_pe_v[0]), cmp_w1_v[0].astype(BF16),
                   cmp_w2_v[0].astype(BF16))
    o_b = _nsa_attention(qkv_b, kc, vc, gates, slopes_b, _overlap_t(S // SEL_BLOCK, kc.shape[2]), B, S)

    h, hn2 = _out_projection([o for o, _ in dil], [s for _, s in dil], o_b, x2,
                             grp_norm_a[0][None, :], grp_norm_b[0][None, :],
                             w_out[0].astype(BF16), norm2_g[0][None, :])
    out = _ffn(hn2, h, w_gate[0].astype(BF16), w_up[0].astype(BF16), w_down[0].astype(BF16),
               final_g[None, :])
    return out.reshape(B, S, D)
```

```python
import functools
import math

import jax
import jax.numpy as jnp
from jax import lax
from jax.experimental import pallas as pl
from jax.experimental.pallas import tpu as pltpu

F32 = jnp.float32
BF16 = jnp.bfloat16

D_MODEL = 2048
HEAD_DIM = 128
N_HEADS = 16
N_HEADS_A = 8
N_HEADS_B = 8
WIDTH_A = N_HEADS_A * HEAD_DIM
WIDTH_B = N_HEADS_B * HEAD_DIM
DIL_CONFIGS = ((128, 1), (512, 4), (2048, 16))
DIL_BLOCK = 128
NSA_KV_GROUPS = 2
NSA_HPG = N_HEADS_B // NSA_KV_GROUPS
KV_WIDTH_B = NSA_KV_GROUPS * HEAD_DIM
CMP_BLOCK = 32
CMP_STRIDE = 16
CMP_HIDDEN = 2 * HEAD_DIM
SEL_BLOCK = 64
SEL_TOP_N = 16
SEL_FORCED_LOCAL = 2
FORCE_BONUS = 1.0e3
WIN_SIZE = 512
N_GATES = 3
FFN_HIDDEN = 5632
RMS_EPS = 1e-6
NEG_INF = -1e30

COLS_A = 3 * WIDTH_A
COLS_B = WIDTH_B + 6 * KV_WIDTH_B
LANE = 128
GATE_PAD = NSA_KV_GROUPS * LANE

VMEM_LIMIT = 56 * 1024 * 1024


def _cparams(sem):
    return pltpu.CompilerParams(dimension_semantics=sem, vmem_limit_bytes=VMEM_LIMIT)


def _nt_dot(a, b):
    return lax.dot_general(a, b, (((1,), (1,)), ((), ())), preferred_element_type=F32)


IN_TM = 1024
IN_TN = 512
N_TILES_A = COLS_A // IN_TN
N_TILES_B = COLS_B // IN_TN
Q_B_TILE0 = N_TILES_A


def _inproj_kernel(x_ref, g_ref, w_ref, wg_ref, oa_ref, ob_ref, og_ref, xn_ref, *, scale):
    j = pl.program_id(1)

    @pl.when(j == 0)
    def _():
        x = x_ref[...]
        y = x * lax.rsqrt(jnp.mean(x * x, axis=-1, keepdims=True) + RMS_EPS)
        xn = (y * g_ref[...]).astype(BF16)
        xn_ref[...] = xn
        og_ref[...] = jnp.dot(xn, wg_ref[...], preferred_element_type=F32)

    acc = jnp.dot(xn_ref[...], w_ref[...], preferred_element_type=F32)
    is_q = (j < 2) | ((j >= Q_B_TILE0) & (j < Q_B_TILE0 + 2))
    acc = acc * jnp.where(is_q, scale, 1.0).astype(F32)

    @pl.when(j < N_TILES_A)
    def _():
        oa_ref[...] = acc.astype(BF16)

    @pl.when(j >= N_TILES_A)
    def _():
        ob_ref[...] = acc.astype(BF16)


def _in_projection(x2, g, w_main, w_gate):
    T = x2.shape[0]
    grid = (T // IN_TM, N_TILES_A + N_TILES_B)
    return pl.pallas_call(
        functools.partial(_inproj_kernel, scale=1.0 / math.sqrt(HEAD_DIM)),
        grid=grid,
        in_specs=[
            pl.BlockSpec((IN_TM, D_MODEL), lambda i, j: (i, 0)),
            pl.BlockSpec((1, D_MODEL), lambda i, j: (0, 0)),
            pl.BlockSpec((D_MODEL, IN_TN), lambda i, j: (0, j)),
            pl.BlockSpec((D_MODEL, GATE_PAD), lambda i, j: (0, 0)),
        ],
        out_specs=[
            pl.BlockSpec((IN_TM, IN_TN), lambda i, j: (i, jnp.minimum(j, N_TILES_A - 1))),
            pl.BlockSpec((IN_TM, IN_TN), lambda i, j: (i, jnp.maximum(j - N_TILES_A, 0))),
            pl.BlockSpec((IN_TM, GATE_PAD), lambda i, j: (i, 0)),
        ],
        out_shape=[
            jax.ShapeDtypeStruct((T, COLS_A), BF16),
            jax.ShapeDtypeStruct((T, COLS_B), BF16),
            jax.ShapeDtypeStruct((T, GATE_PAD), F32),
        ],
        scratch_shapes=[pltpu.VMEM((IN_TM, D_MODEL), BF16)],
        compiler_params=_cparams(("arbitrary", "arbitrary")),
        name="in_projection",
    )(x2, g, w_main, w_gate)


def _dilated_kernel(slopes_ref, q_ref, kc_ref, kp_ref, vc_ref, vp_ref, o_ref, lse_ref, *,
                    dilation, span):
    n = pl.program_id(2)
    row = lax.broadcasted_iota(jnp.int32, (DIL_BLOCK, DIL_BLOCK), 0)
    col = lax.broadcasted_iota(jnp.int32, (DIL_BLOCK, DIL_BLOCK), 1)
    rel_c = row - col
    rel_p = rel_c + DIL_BLOCK
    mask_c = rel_c >= 0
    mask_p = (rel_p <= span) & (n > 0)
    dist_c = (rel_c * dilation).astype(F32)
    dist_p = (rel_p * dilation).astype(F32)
    lane = lax.broadcasted_iota(jnp.int32, (DIL_BLOCK, LANE), 1)
    lse_tile = jnp.zeros((DIL_BLOCK, LANE), F32)
    for h in range(N_HEADS_A):
        hs = slice(h * HEAD_DIM, (h + 1) * HEAD_DIM)
        slope = slopes_ref[h]
        q = q_ref[0, :, hs]
        s_c = _nt_dot(q, kc_ref[0, :, hs])
        s_p = _nt_dot(q, kp_ref[0, :, hs])
        s_c = jnp.where(mask_c, s_c - slope * dist_c, NEG_INF)
        s_p = jnp.where(mask_p, s_p - slope * dist_p, NEG_INF)
        m = jnp.maximum(jnp.max(s_c, axis=-1, keepdims=True), jnp.max(s_p, axis=-1, keepdims=True))
        e_c = jnp.exp(s_c - m)
        e_p = jnp.exp(s_p - m)
        l = jnp.sum(e_c, axis=-1, keepdims=True) + jnp.sum(e_p, axis=-1, keepdims=True)
        acc = jnp.dot(e_c.astype(BF16), vc_ref[0, :, hs], preferred_element_type=F32)
        acc = acc + jnp.dot(e_p.astype(BF16), vp_ref[0, :, hs], preferred_element_type=F32)
        o_ref[0, :, hs] = acc / l
        lse_tile = jnp.where(lane == h, m + jnp.log(l), lse_tile)
    lse_ref[0] = lse_tile


def _dilated_attention(qkv_a, slopes_a, B, S, window, dilation):
    L = S // dilation
    nb = L // DIL_BLOCK
    a3 = qkv_a.reshape(B, L, dilation * COLS_A)
    blk = (1, DIL_BLOCK, WIDTH_A)
    prev = lambda n: jnp.maximum(n - 1, 0)
    o, lse = pl.pallas_call(
        functools.partial(_dilated_kernel, dilation=dilation, span=window // dilation),
        grid=(B, dilation, nb),
        in_specs=[
            pl.BlockSpec(memory_space=pltpu.SMEM),
            pl.BlockSpec(blk, lambda b, r, n: (b, n, 3 * r)),
            pl.BlockSpec(blk, lambda b, r, n: (b, n, 3 * r + 1)),
            pl.BlockSpec(blk, lambda b, r, n: (b, prev(n), 3 * r + 1)),
            pl.BlockSpec(blk, lambda b, r, n: (b, n, 3 * r + 2)),
            pl.BlockSpec(blk, lambda b, r, n: (b, prev(n), 3 * r + 2)),
        ],
        out_specs=[
            pl.BlockSpec(blk, lambda b, r, n: (b, n, r)),
            pl.BlockSpec((1, DIL_BLOCK, LANE), lambda b, r, n: (b, n, r)),
        ],
        out_shape=[
            jax.ShapeDtypeStruct((B, L, dilation * WIDTH_A), F32),
            jax.ShapeDtypeStruct((B, L, dilation * LANE), F32),
        ],
        compiler_params=_cparams(("arbitrary", "arbitrary", "arbitrary")),
        name=f"dilated_attention_d{dilation}",
    )(slopes_a, a3, a3, a3, a3, a3)
    return o.reshape(B * S, WIDTH_A), lse.reshape(B * S, LANE)


CHUNK_FLAT = CMP_STRIDE * HEAD_DIM


def _compress_kernel(x_ref, pe_ref, w1_ref, w2_ref, o_ref):
    x = x_ref[0, 0].astype(F32)
    first = jnp.dot((x + pe_ref[0:1, :]).astype(BF16), w1_ref[0:CHUNK_FLAT, :],
                    preferred_element_type=F32)
    second = jnp.dot((x + pe_ref[1:2, :]).astype(BF16), w1_ref[CHUNK_FLAT:2 * CHUNK_FLAT, :],
                     preferred_element_type=F32)
    nc = x.shape[0]
    hidden = first + pltpu.roll(second, shift=nc - 1, axis=0)
    act = hidden * jax.nn.sigmoid(hidden)
    o_ref[0, 0] = jnp.dot(act.astype(BF16), w2_ref[...], preferred_element_type=F32).astype(BF16)


def _compress(chunks, pe2, w1, w2):
    B, G, nc, _ = chunks.shape
    return pl.pallas_call(
        _compress_kernel,
        grid=(B, G),
        in_specs=[
            pl.BlockSpec((1, 1, nc, CHUNK_FLAT), lambda b, g: (b, g, 0, 0)),
            pl.BlockSpec((2, CHUNK_FLAT), lambda b, g: (0, 0)),
            pl.BlockSpec((2 * CHUNK_FLAT, CMP_HIDDEN), lambda b, g: (0, 0)),
            pl.BlockSpec((CMP_HIDDEN, HEAD_DIM), lambda b, g: (0, 0)),
        ],
        out_specs=pl.BlockSpec((1, 1, nc, HEAD_DIM), lambda b, g: (b, g, 0, 0)),
        out_shape=jax.ShapeDtypeStruct((B, G, nc, HEAD_DIM), BF16),
        compiler_params=_cparams(("arbitrary", "arbitrary")),
        name="nsa_compress",
    )(chunks, pe2, w1, w2)


NSA_TQ = 256
NSA_TK = 512
WIN_SPAN = WIN_SIZE + NSA_TQ
SLOPE_PIECES = 3


def _stack_heads(a):
    return jnp.concatenate([a[:, h * HEAD_DIM:(h + 1) * HEAD_DIM] for h in range(NSA_HPG)], axis=0)


def _nsa_kernel(slopes_ref, q_ref, kc_ref, vct_ref, ks_ref, vst_ref, kw_ref, vwt_ref, kf_ref, gate_ref,
                ovl_ref, o_ref, *, n_sel):
    g = pl.program_id(1)
    qi = pl.program_id(2)
    t0 = qi * NSA_TQ
    cols = NSA_HPG * NSA_TQ

    q = _stack_heads(q_ref[0])
    slope_row = jnp.concatenate(
        [jnp.full((1, NSA_TQ), slopes_ref[SLOPE_PIECES, g * NSA_HPG + h], F32) for h in range(NSA_HPG)],
        axis=1)
    t_row = t0 + (lax.broadcasted_iota(jnp.int32, (1, cols), 1) & (NSA_TQ - 1))

    n_cmp_pad = kc_ref.shape[2]
    cend = (lax.broadcasted_iota(jnp.int32, (n_cmp_pad, 1), 0) * CMP_STRIDE + (CMP_BLOCK - 1))
    rel_c = t_row - cend
    ok_c = rel_c >= 0
    s = _nt_dot(kc_ref[0, 0], q)
    s = s - slope_row * rel_c.astype(F32)
    s = jnp.where(ok_c, s, NEG_INF)
    m = jnp.max(s, axis=0, keepdims=True)
    e = jnp.where(ok_c, jnp.exp(s - m), 0.0)
    p = e / jnp.maximum(jnp.sum(e, axis=0, keepdims=True), 1e-30)
    o_cmp = jnp.dot(vct_ref[0, 0], p.astype(BF16), preferred_element_type=F32)

    p_grp = p[:, 0:NSA_TQ]
    for h in range(1, NSA_HPG):
        p_grp = p_grp + p[:, h * NSA_TQ:(h + 1) * NSA_TQ]
    p_hi = p_grp.astype(BF16)
    p_lo = (p_grp - p_hi.astype(F32)).astype(BF16)
    ovl = ovl_ref[...]
    imp_t = (jnp.dot(ovl, p_hi, preferred_element_type=F32)
             + jnp.dot(ovl, p_lo, preferred_element_type=F32))
    blk = lax.broadcasted_iota(jnp.int32, (n_sel, NSA_TQ), 0)
    cur = (t0 + lax.broadcasted_iota(jnp.int32, (n_sel, NSA_TQ), 1)) // SEL_BLOCK
    valid = blk <= cur
    forced = (blk == 0) | (valid & (blk > cur - SEL_FORCED_LOCAL))
    score = jnp.where(valid, imp_t + jnp.where(forced, FORCE_BONUS, 0.0), -1.0)
    rank = jnp.zeros((n_sel, NSA_TQ), jnp.int32)
    for mth in range(n_sel):
        other = score[mth:mth + 1, :]
        tie_first = jnp.where(blk > mth, 1, 0)
        rank = rank + jnp.where(other > score, 1, jnp.where(other == score, tie_first, 0))
    member_t = jnp.where(valid, jnp.where(rank < SEL_TOP_N, 1.0, 0.0), 0.0).astype(BF16)
    qrow = lax.broadcasted_iota(jnp.int32, (cols, NSA_TQ), 0) & (NSA_TQ - 1)
    eye = jnp.where(qrow == lax.broadcasted_iota(jnp.int32, (cols, NSA_TQ), 1), 1.0, 0.0).astype(BF16)
    member_pad = jnp.concatenate([member_t, jnp.ones((LANE - n_sel, NSA_TQ), BF16)], axis=0)
    picked = _nt_dot(eye, member_pad)
    lane = lax.broadcasted_iota(jnp.int32, (1, LANE), 1)
    slope_feat = []
    for h in range(NSA_HPG):
        row = jnp.zeros((1, LANE), F32)
        for part in range(2):
            for piece in range(SLOPE_PIECES):
                row = jnp.where(lane == n_sel + part * SLOPE_PIECES + piece,
                                slopes_ref[piece, g * NSA_HPG + h], row)
        slope_feat.append(jnp.broadcast_to(row, (NSA_TQ, LANE)))
    slope_feat = jnp.concatenate(slope_feat, axis=0)
    q_sel = jnp.concatenate([q, (jnp.where(picked > 0.5, 0.0, NEG_INF) + slope_feat).astype(BF16)], axis=1)
    q_win = jnp.concatenate([q, slope_feat.astype(BF16)], axis=1)

    def keys_aug(k_ref, start, size):
        return jnp.concatenate([k_ref[0, pl.ds(start, size), :], kf_ref[pl.ds(start, size), :]], axis=1)

    def sel_step(j, carry, causal):
        m_i, l_i, acc = carry
        start = pl.multiple_of(j * NSA_TK, NSA_TK)
        s_ = _nt_dot(keys_aug(ks_ref, start, NSA_TK), q_sel)
        if causal:
            pos = start + lax.broadcasted_iota(jnp.int32, (NSA_TK, 1), 0)
            s_ = jnp.where(pos <= t_row, s_, NEG_INF)
        m_new = jnp.maximum(m_i, jnp.max(s_, axis=0, keepdims=True))
        alpha = jnp.exp(m_i - m_new)
        p_ = jnp.exp(s_ - m_new)
        l_new = alpha * l_i + jnp.sum(p_, axis=0, keepdims=True)
        acc_new = alpha * acc + jnp.dot(vst_ref[0, 0, :, pl.ds(start, NSA_TK)], p_.astype(BF16),
                                        preferred_element_type=F32)
        return m_new, l_new, acc_new

    init = (jnp.full((1, cols), NEG_INF, F32), jnp.zeros((1, cols), F32),
            jnp.zeros((HEAD_DIM, cols), F32))
    last = (t0 + NSA_TQ - 1) // NSA_TK
    carry = lax.fori_loop(0, last, functools.partial(sel_step, causal=False), init)
    _, l_s, acc_s = sel_step(last, carry, causal=True)
    o_sel = acc_s / l_s

    def window(wstart, edge_only):
        pos_w = wstart + lax.broadcasted_iota(jnp.int32, (WIN_SPAN, 1), 0)
        s_w = _nt_dot(keys_aug(kw_ref, wstart, WIN_SPAN), q_win)
        rel_w = t_row - pos_w
        if edge_only:
            head = jnp.where(rel_w[:NSA_TQ] < WIN_SIZE, s_w[:NSA_TQ], NEG_INF)
            tail = jnp.where(rel_w[WIN_SIZE:] >= 0, s_w[WIN_SIZE:], NEG_INF)
            s_w = jnp.concatenate([head, s_w[NSA_TQ:WIN_SIZE], tail], axis=0)
        else:
            s_w = jnp.where(rel_w >= 0, jnp.where(rel_w < WIN_SIZE, s_w, NEG_INF), NEG_INF)
        p_w = jnp.exp(s_w - jnp.max(s_w, axis=0, keepdims=True))
        return (jnp.dot(vwt_ref[0, 0, :, pl.ds(wstart, WIN_SPAN)], p_w.astype(BF16),
                        preferred_element_type=F32) / jnp.sum(p_w, axis=0, keepdims=True))

    o_win = lax.cond(t0 >= WIN_SIZE,
                     lambda: window(pl.multiple_of(t0 - WIN_SIZE, NSA_TQ), True),
                     lambda: window(0, False))

    gates_t = jax.nn.sigmoid(gate_ref[0]).T
    for h in range(NSA_HPG):
        cs = slice(h * NSA_TQ, (h + 1) * NSA_TQ)
        gsel = [gates_t[h * N_GATES + k:h * N_GATES + k + 1, :] for k in range(N_GATES)]
        mix_t = gsel[0] * o_cmp[:, cs] + gsel[1] * o_sel[:, cs] + gsel[2] * o_win[:, cs]
        o_ref[0, :, h * HEAD_DIM:(h + 1) * HEAD_DIM] = mix_t.T


def _key_features(S):
    pos = jnp.arange(S)[:, None]
    lane = jnp.arange(LANE)[None, :]
    n_sel = S // SEL_BLOCK
    feat = jnp.where(lane == pos // SEL_BLOCK, 1, 0)
    feat = jnp.where((lane >= n_sel) & (lane < n_sel + SLOPE_PIECES), pos % SEL_BLOCK, feat)
    feat = jnp.where((lane >= n_sel + SLOPE_PIECES) & (lane < n_sel + 2 * SLOPE_PIECES),
                     (pos // SEL_BLOCK) * SEL_BLOCK, feat)
    return feat.astype(BF16)


def _slope_table(slopes):
    rows, rest = [], slopes
    for _ in range(SLOPE_PIECES):
        piece = rest.astype(BF16).astype(F32)
        rows.append(piece)
        rest = rest - piece
    return jnp.stack(rows + [slopes])


def _nsa_attention(qkv_b, kc, vc_t, vs_t, vw_t, gates, slopes_b, overlap_t, B, S):
    assert S % NSA_TK == 0 and S >= WIN_SPAN
    assert S // SEL_BLOCK <= SEL_BLOCK and S // SEL_BLOCK + 2 * SLOPE_PIECES <= LANE
    b3 = qkv_b.reshape(B, S, COLS_B)
    g3 = gates.reshape(B, S, GATE_PAD)
    n_cmp_pad = kc.shape[2]
    n_sel = S // SEL_BLOCK
    q_w = NSA_HPG * HEAD_DIM
    kv0 = WIDTH_B // HEAD_DIM
    per = KV_WIDTH_B // HEAD_DIM
    seq_blk = (1, S, HEAD_DIM)
    seq_t_blk = (1, 1, HEAD_DIM, S)
    grp = lambda b, g, i: (b, g, 0, 0)
    out = pl.pallas_call(
        functools.partial(_nsa_kernel, n_sel=n_sel),
        grid=(B, NSA_KV_GROUPS, S // NSA_TQ),
        in_specs=[
            pl.BlockSpec(memory_space=pltpu.SMEM),
            pl.BlockSpec((1, NSA_TQ, q_w), lambda b, g, i: (b, i, g)),
            pl.BlockSpec((1, 1, n_cmp_pad, HEAD_DIM), grp),
            pl.BlockSpec((1, 1, HEAD_DIM, n_cmp_pad), grp),
            pl.BlockSpec(seq_blk, lambda b, g, i: (b, 0, kv0 + 2 * per + g)),
            pl.BlockSpec(seq_t_blk, grp),
            pl.BlockSpec(seq_blk, lambda b, g, i: (b, 0, kv0 + 4 * per + g)),
            pl.BlockSpec(seq_t_blk, grp),
            pl.BlockSpec((S, LANE), lambda b, g, i: (0, 0)),
            pl.BlockSpec((1, NSA_TQ, LANE), lambda b, g, i: (b, i, g)),
            pl.BlockSpec((n_sel, n_cmp_pad), lambda b, g, i: (0, 0)),
        ],
        out_specs=pl.BlockSpec((1, NSA_TQ, q_w), lambda b, g, i: (b, i, g)),
        out_shape=jax.ShapeDtypeStruct((B, S, WIDTH_B), F32),
        compiler_params=_cparams(("arbitrary", "arbitrary", "arbitrary")),
        name="nsa_attention",
    )(_slope_table(slopes_b), b3, kc, vc_t, b3, vs_t, b3, vw_t, _key_features(S), g3, overlap_t)
    return out.reshape(B * S, WIDTH_B)


OUT_TM = 256


def _rms(x, g):
    return x * lax.rsqrt(jnp.mean(x * x, axis=-1, keepdims=True) + RMS_EPS) * g


def _outproj_kernel(o1_ref, o2_ref, o3_ref, l1_ref, l2_ref, l3_ref, ob_ref, x_ref, ga_ref, gb_ref,
                    w_ref, g2_ref, h_ref, hn_ref):
    l1, l2, l3 = l1_ref[...], l2_ref[...], l3_ref[...]
    mx = jnp.maximum(jnp.maximum(l1, l2), l3)
    e1, e2, e3 = jnp.exp(l1 - mx), jnp.exp(l2 - mx), jnp.exp(l3 - mx)
    den = e1 + e2 + e3
    w1, w2, w3 = e1 / den, e2 / den, e3 / den
    parts = []
    for h in range(N_HEADS_A):
        hs = slice(h * HEAD_DIM, (h + 1) * HEAD_DIM)
        parts.append(w1[:, h:h + 1] * o1_ref[:, hs] + w2[:, h:h + 1] * o2_ref[:, hs]
                     + w3[:, h:h + 1] * o3_ref[:, hs])
    o_a = jnp.concatenate(parts, axis=-1)
    mixed = jnp.concatenate([_rms(o_a, ga_ref[...]), _rms(ob_ref[...], gb_ref[...])], axis=-1)
    h = x_ref[...] + jnp.dot(mixed.astype(BF16), w_ref[...], preferred_element_type=F32)
    h_ref[...] = h
    hn_ref[...] = _rms(h, g2_ref[...]).astype(BF16)


def _out_projection(o_as, lses, o_b, x2, ga, gb, w_out, g2):
    T = x2.shape[0]
    row = lambda w: pl.BlockSpec((OUT_TM, w), lambda i: (i, 0))
    full = lambda r, c: pl.BlockSpec((r, c), lambda i: (0, 0))
    return pl.pallas_call(
        _outproj_kernel,
        grid=(T // OUT_TM,),
        in_specs=[row(WIDTH_A)] * 3 + [row(LANE)] * 3 + [row(WIDTH_B), row(D_MODEL),
                  full(1, WIDTH_A), full(1, WIDTH_B), full(WIDTH_A + WIDTH_B, D_MODEL), full(1, D_MODEL)],
        out_specs=[row(D_MODEL), row(D_MODEL)],
        out_shape=[jax.ShapeDtypeStruct((T, D_MODEL), F32), jax.ShapeDtypeStruct((T, D_MODEL), BF16)],
        compiler_params=_cparams(("arbitrary",)),
        name="out_projection",
    )(*o_as, *lses, o_b, x2, ga, gb, w_out, g2)


FFN_TM = 512
FFN_TH = 512


def _ffn_kernel(hn_ref, h_ref, wg_ref, wu_ref, wd_ref, gf_ref, o_ref, acc_ref):
    j = pl.program_id(1)
    hn = hn_ref[...]
    gate = jnp.dot(hn, wg_ref[...], preferred_element_type=F32)
    up = jnp.dot(hn, wu_ref[...], preferred_element_type=F32)
    act = (gate * jax.nn.sigmoid(gate) * up).astype(BF16)
    part = jnp.dot(act, wd_ref[...], preferred_element_type=F32)

    @pl.when(j == 0)
    def _():
        acc_ref[...] = part

    @pl.when(j > 0)
    def _():
        acc_ref[...] += part

    @pl.when(j == pl.num_programs(1) - 1)
    def _():
        o_ref[...] = _rms(h_ref[...] + acc_ref[...], gf_ref[...])


def _ffn(hn2, h, w_gate, w_up, w_down, gf):
    T = h.shape[0]
    return pl.pallas_call(
        _ffn_kernel,
        grid=(T // FFN_TM, FFN_HIDDEN // FFN_TH),
        in_specs=[
            pl.BlockSpec((FFN_TM, D_MODEL), lambda i, j: (i, 0)),
            pl.BlockSpec((FFN_TM, D_MODEL), lambda i, j: (i, 0)),
            pl.BlockSpec((D_MODEL, FFN_TH), lambda i, j: (0, j)),
            pl.BlockSpec((D_MODEL, FFN_TH), lambda i, j: (0, j)),
            pl.BlockSpec((FFN_TH, D_MODEL), lambda i, j: (j, 0)),
            pl.BlockSpec((1, D_MODEL), lambda i, j: (0, 0)),
        ],
        out_specs=pl.BlockSpec((FFN_TM, D_MODEL), lambda i, j: (i, 0)),
        out_shape=jax.ShapeDtypeStruct((T, D_MODEL), F32),
        scratch_shapes=[pltpu.VMEM((FFN_TM, D_MODEL), F32)],
        compiler_params=_cparams(("arbitrary", "arbitrary")),
        name="swiglu_ffn",
    )(hn2, h, w_gate, w_up, w_down, gf)


def _chunk_view(qkv_b3, col0):
    B, S, _ = qkv_b3.shape
    a = qkv_b3[:, :, col0:col0 + KV_WIDTH_B]
    a = a.reshape(B, S // CMP_STRIDE, CMP_STRIDE, NSA_KV_GROUPS, HEAD_DIM)
    return a.transpose(0, 3, 1, 2, 4).reshape(B, NSA_KV_GROUPS, S // CMP_STRIDE, CHUNK_FLAT)


def _overlap_t(n_sel, n_cmp_pad):
    cstart = jnp.arange(n_cmp_pad)[None, :] * CMP_STRIDE
    sstart = jnp.arange(n_sel)[:, None] * SEL_BLOCK
    return ((cstart < sstart + SEL_BLOCK) & (cstart + CMP_BLOCK > sstart)).astype(BF16)


def kernel(x, norm1_g, w_in, cmp_pe_k, cmp_w1_k, cmp_w2_k, cmp_pe_v, cmp_w1_v, cmp_w2_v, grp_norm_a,
           grp_norm_b, w_out, norm2_g, w_gate, w_up, w_down, final_g):
    B, S, D = x.shape
    assert D == D_MODEL and S % (DIL_BLOCK * max(d for _, d in DIL_CONFIGS)) == 0
    assert (B * S) % IN_TM == 0 and w_in.shape[0] == 1, "single-layer block only"
    T = B * S
    slopes = jnp.exp2(-8.0 * jnp.arange(1, N_HEADS + 1, dtype=F32) / N_HEADS)
    slopes_a, slopes_b = slopes[0::2], slopes[1::2]
    n_main = COLS_A + COLS_B
    per_group = NSA_HPG * N_GATES

    x2 = x.reshape(T, D)
    w_main = w_in[0][:, :n_main].astype(BF16)
    w_g = w_in[0][:, n_main:].reshape(D, NSA_KV_GROUPS, per_group)
    w_g = jnp.pad(w_g, ((0, 0), (0, 0), (0, LANE - per_group))).reshape(D, GATE_PAD).astype(BF16)
    qkv_a, qkv_b, gates = _in_projection(x2, norm1_g[0][None, :], w_main, w_g)

    dil = [_dilated_attention(qkv_a, slopes_a, B, S, w, d) for w, d in DIL_CONFIGS]

    b3 = qkv_b.reshape(B, S, COLS_B)
    pe2 = lambda pe: pe.reshape(2, CHUNK_FLAT)
    kc = _compress(_chunk_view(b3, WIDTH_B), pe2(cmp_pe_k[0]), cmp_w1_k[0].astype(BF16),
                   cmp_w2_k[0].astype(BF16))
    vc = _compress(_chunk_view(b3, WIDTH_B + KV_WIDTH_B), pe2(cmp_pe_v[0]), cmp_w1_v[0].astype(BF16),
                   cmp_w2_v[0].astype(BF16))
    col_vs = WIDTH_B + 3 * KV_WIDTH_B
    col_vw = WIDTH_B + 5 * KV_WIDTH_B
    seq_t = lambda c0: b3[:, :, c0:c0 + KV_WIDTH_B].reshape(B, S, NSA_KV_GROUPS, HEAD_DIM).transpose(0, 2, 3, 1)
    o_b = _nsa_attention(qkv_b, kc, vc.transpose(0, 1, 3, 2), seq_t(col_vs), seq_t(col_vw), gates, slopes_b,
                         _overlap_t(S // SEL_BLOCK, kc.shape[2]), B, S)

    h, hn2 = _out_projection([o for o, _ in dil], [s for _, s in dil], o_b, x2,
                             grp_norm_a[0][None, :], grp_norm_b[0][None, :],
                             w_out[0].astype(BF16), norm2_g[0][None, :])
    out = _ffn(hn2, h, w_gate[0].astype(BF16), w_up[0].astype(BF16), w_down[0].astype(BF16),
               final_g[None, :])
    return out.reshape(B, S, D)
```

```python
import functools
import math

import jax
import jax.numpy as jnp
from jax import lax
from jax.experimental import pallas as pl
from jax.experimental.pallas import tpu as pltpu

F32 = jnp.float32
BF16 = jnp.bfloat16

D_MODEL = 2048
HEAD_DIM = 128
N_HEADS = 16
N_HEADS_A = 8
N_HEADS_B = 8
WIDTH_A = N_HEADS_A * HEAD_DIM
WIDTH_B = N_HEADS_B * HEAD_DIM
DIL_CONFIGS = ((128, 1), (512, 4), (2048, 16))
DIL_BLOCK = 128
NSA_KV_GROUPS = 2
NSA_HPG = N_HEADS_B // NSA_KV_GROUPS
KV_WIDTH_B = NSA_KV_GROUPS * HEAD_DIM
CMP_BLOCK = 32
CMP_STRIDE = 16
CMP_HIDDEN = 2 * HEAD_DIM
SEL_BLOCK = 64
SEL_TOP_N = 16
SEL_FORCED_LOCAL = 2
FORCE_BONUS = 1.0e3
WIN_SIZE = 512
N_GATES = 3
FFN_HIDDEN = 5632
RMS_EPS = 1e-6
NEG_INF = -1e30

COLS_A = 3 * WIDTH_A
COLS_B = WIDTH_B + 6 * KV_WIDTH_B
LANE = 128
GATE_PAD = NSA_KV_GROUPS * LANE

VMEM_LIMIT = 56 * 1024 * 1024


def _cparams(sem):
    return pltpu.CompilerParams(dimension_semantics=sem, vmem_limit_bytes=VMEM_LIMIT)


def _nt_dot(a, b):
    return lax.dot_general(a, b, (((1,), (1,)), ((), ())), preferred_element_type=F32)


IN_TM = 1024
IN_TN = 512
N_TILES_A = COLS_A // IN_TN
N_TILES_B = COLS_B // IN_TN
Q_B_TILE0 = N_TILES_A


def _inproj_kernel(x_ref, g_ref, w_ref, wg_ref, oa_ref, ob_ref, og_ref, xn_ref, *, scale):
    j = pl.program_id(1)

    @pl.when(j == 0)
    def _():
        x = x_ref[...]
        y = x * lax.rsqrt(jnp.mean(x * x, axis=-1, keepdims=True) + RMS_EPS)
        xn = (y * g_ref[...]).astype(BF16)
        xn_ref[...] = xn
        og_ref[...] = jnp.dot(xn, wg_ref[...], preferred_element_type=F32)

    acc = jnp.dot(xn_ref[...], w_ref[...], preferred_element_type=F32)
    is_q = (j < 2) | ((j >= Q_B_TILE0) & (j < Q_B_TILE0 + 2))
    acc = acc * jnp.where(is_q, scale, 1.0).astype(F32)

    @pl.when(j < N_TILES_A)
    def _():
        oa_ref[...] = acc.astype(BF16)

    @pl.when(j >= N_TILES_A)
    def _():
        ob_ref[...] = acc.astype(BF16)


def _in_projection(x2, g, w_main, w_gate):
    T = x2.shape[0]
    grid = (T // IN_TM, N_TILES_A + N_TILES_B)
    return pl.pallas_call(
        functools.partial(_inproj_kernel, scale=1.0 / math.sqrt(HEAD_DIM)),
        grid=grid,
        in_specs=[
            pl.BlockSpec((IN_TM, D_MODEL), lambda i, j: (i, 0)),
            pl.BlockSpec((1, D_MODEL), lambda i, j: (0, 0)),
            pl.BlockSpec((D_MODEL, IN_TN), lambda i, j: (0, j)),
            pl.BlockSpec((D_MODEL, GATE_PAD), lambda i, j: (0, 0)),
        ],
        out_specs=[
            pl.BlockSpec((IN_TM, IN_TN), lambda i, j: (i, jnp.minimum(j, N_TILES_A - 1))),
            pl.BlockSpec((IN_TM, IN_TN), lambda i, j: (i, jnp.maximum(j - N_TILES_A, 0))),
            pl.BlockSpec((IN_TM, GATE_PAD), lambda i, j: (i, 0)),
        ],
        out_shape=[
            jax.ShapeDtypeStruct((T, COLS_A), BF16),
            jax.ShapeDtypeStruct((T, COLS_B), BF16),
            jax.ShapeDtypeStruct((T, GATE_PAD), F32),
        ],
        scratch_shapes=[pltpu.VMEM((IN_TM, D_MODEL), BF16)],
        compiler_params=_cparams(("arbitrary", "arbitrary")),
        name="in_projection",
    )(x2, g, w_main, w_gate)


def _dilated_kernel(slopes_ref, q_ref, kc_ref, kp_ref, vc_ref, vp_ref, o_ref, lse_ref, *,
                    dilation, span):
    n = pl.program_id(2)
    key = lax.broadcasted_iota(jnp.int32, (2 * DIL_BLOCK, DIL_BLOCK), 0)
    qry = lax.broadcasted_iota(jnp.int32, (2 * DIL_BLOCK, DIL_BLOCK), 1)
    rel = qry + DIL_BLOCK - key
    in_prev = jnp.where(rel <= span, jnp.where(n > 0, 0.0, NEG_INF), NEG_INF)
    mask_bias = jnp.where(key < DIL_BLOCK, in_prev, jnp.where(rel >= 0, 0.0, NEG_INF))
    dist = (rel * dilation).astype(F32)
    heads = [slice(h * HEAD_DIM, (h + 1) * HEAD_DIM) for h in range(N_HEADS_A)]
    scores = [_nt_dot(jnp.concatenate([kp_ref[0, :, hs], kc_ref[0, :, hs]], axis=0), q_ref[0, :, hs])
              for hs in heads]
    scores = [s - slopes_ref[h] * dist + mask_bias for h, s in enumerate(scores)]
    maxes = [jnp.max(s, axis=0, keepdims=True) for s in scores]
    probs = [jnp.exp(s - m) for s, m in zip(scores, maxes)]
    sums = [jnp.sum(p, axis=0, keepdims=True) for p in probs]
    for hs, p, l in zip(heads, probs, sums):
        v = jnp.concatenate([vp_ref[0, :, hs], vc_ref[0, :, hs]], axis=0)
        o_ref[0, :, hs] = lax.dot_general((p * (1.0 / l)).astype(BF16), v, (((0,), (0,)), ((), ())),
                                          preferred_element_type=F32)
    lse_rows = jnp.concatenate([m + jnp.log(l) for m, l in zip(maxes, sums)]
                               + [jnp.zeros((LANE - N_HEADS_A, DIL_BLOCK), F32)], axis=0)
    lse_ref[0] = lse_rows.T


def _dilated_attention(qkv_a, slopes_a, B, S, window, dilation):
    L = S // dilation
    nb = L // DIL_BLOCK
    a3 = qkv_a.reshape(B, L, dilation * COLS_A)
    blk = (1, DIL_BLOCK, WIDTH_A)
    prev = lambda n: jnp.maximum(n - 1, 0)
    o, lse = pl.pallas_call(
        functools.partial(_dilated_kernel, dilation=dilation, span=window // dilation),
        grid=(B, dilation, nb),
        in_specs=[
            pl.BlockSpec(memory_space=pltpu.SMEM),
            pl.BlockSpec(blk, lambda b, r, n: (b, n, 3 * r)),
            pl.BlockSpec(blk, lambda b, r, n: (b, n, 3 * r + 1)),
            pl.BlockSpec(blk, lambda b, r, n: (b, prev(n), 3 * r + 1)),
            pl.BlockSpec(blk, lambda b, r, n: (b, n, 3 * r + 2)),
            pl.BlockSpec(blk, lambda b, r, n: (b, prev(n), 3 * r + 2)),
        ],
        out_specs=[
            pl.BlockSpec(blk, lambda b, r, n: (b, n, r)),
            pl.BlockSpec((1, DIL_BLOCK, LANE), lambda b, r, n: (b, n, r)),
        ],
        out_shape=[
            jax.ShapeDtypeStruct((B, L, dilation * WIDTH_A), F32),
            jax.ShapeDtypeStruct((B, L, dilation * LANE), F32),
        ],
        compiler_params=_cparams(("arbitrary", "arbitrary", "arbitrary")),
        name=f"dilated_attention_d{dilation}",
    )(slopes_a, a3, a3, a3, a3, a3)
    return o.reshape(B * S, WIDTH_A), lse.reshape(B * S, LANE)


CHUNK_FLAT = CMP_STRIDE * HEAD_DIM


def _compress_kernel(x_ref, pe_ref, w1_ref, w2_ref, o_ref):
    x = x_ref[0, 0].astype(F32)
    first = jnp.dot((x + pe_ref[0:1, :]).astype(BF16), w1_ref[0:CHUNK_FLAT, :],
                    preferred_element_type=F32)
    second = jnp.dot((x + pe_ref[1:2, :]).astype(BF16), w1_ref[CHUNK_FLAT:2 * CHUNK_FLAT, :],
                     preferred_element_type=F32)
    nc = x.shape[0]
    hidden = first + pltpu.roll(second, shift=nc - 1, axis=0)
    act = hidden * jax.nn.sigmoid(hidden)
    o_ref[0, 0] = jnp.dot(act.astype(BF16), w2_ref[...], preferred_element_type=F32).astype(BF16)


def _compress(chunks, pe2, w1, w2):
    B, G, nc, _ = chunks.shape
    return pl.pallas_call(
        _compress_kernel,
        grid=(B, G),
        in_specs=[
            pl.BlockSpec((1, 1, nc, CHUNK_FLAT), lambda b, g: (b, g, 0, 0)),
            pl.BlockSpec((2, CHUNK_FLAT), lambda b, g: (0, 0)),
            pl.BlockSpec((2 * CHUNK_FLAT, CMP_HIDDEN), lambda b, g: (0, 0)),
            pl.BlockSpec((CMP_HIDDEN, HEAD_DIM), lambda b, g: (0, 0)),
        ],
        out_specs=pl.BlockSpec((1, 1, nc, HEAD_DIM), lambda b, g: (b, g, 0, 0)),
        out_shape=jax.ShapeDtypeStruct((B, G, nc, HEAD_DIM), BF16),
        compiler_params=_cparams(("arbitrary", "arbitrary")),
        name="nsa_compress",
    )(chunks, pe2, w1, w2)


NSA_TQ = 256
NSA_TK = 512
WIN_SPAN = WIN_SIZE + NSA_TQ
SLOPE_PIECES = 3


def _stack_heads(a):
    return jnp.concatenate([a[:, h * HEAD_DIM:(h + 1) * HEAD_DIM] for h in range(NSA_HPG)], axis=0)


def _nsa_kernel(slopes_ref, q_ref, kc_ref, vct_ref, ks_ref, vst_ref, kw_ref, vwt_ref, kf_ref, gate_ref,
                ovl_ref, o_ref, *, n_sel):
    g = pl.program_id(1)
    qi = pl.program_id(2)
    t0 = qi * NSA_TQ
    cols = NSA_HPG * NSA_TQ

    q = _stack_heads(q_ref[0])
    slope_row = jnp.concatenate(
        [jnp.full((1, NSA_TQ), slopes_ref[SLOPE_PIECES, g * NSA_HPG + h], F32) for h in range(NSA_HPG)],
        axis=1)
    t_row = t0 + (lax.broadcasted_iota(jnp.int32, (1, cols), 1) & (NSA_TQ - 1))

    n_cmp_pad = kc_ref.shape[2]
    cend = (lax.broadcasted_iota(jnp.int32, (n_cmp_pad, 1), 0) * CMP_STRIDE + (CMP_BLOCK - 1))
    rel_c = t_row - cend
    ok_c = rel_c >= 0
    s = _nt_dot(kc_ref[0, 0], q)
    s = s - slope_row * rel_c.astype(F32)
    s = jnp.where(ok_c, s, NEG_INF)
    m = jnp.max(s, axis=0, keepdims=True)
    e = jnp.where(ok_c, jnp.exp(s - m), 0.0)
    p = e / jnp.maximum(jnp.sum(e, axis=0, keepdims=True), 1e-30)
    o_cmp = jnp.dot(vct_ref[0, 0], p.astype(BF16), preferred_element_type=F32)

    p_grp = p[:, 0:NSA_TQ]
    for h in range(1, NSA_HPG):
        p_grp = p_grp + p[:, h * NSA_TQ:(h + 1) * NSA_TQ]
    p_hi = p_grp.astype(BF16)
    p_lo = (p_grp - p_hi.astype(F32)).astype(BF16)
    ovl = ovl_ref[...]
    imp_t = (jnp.dot(ovl, p_hi, preferred_element_type=F32)
             + jnp.dot(ovl, p_lo, preferred_element_type=F32))
    blk = lax.broadcasted_iota(jnp.int32, (n_sel, NSA_TQ), 0)
    cur = (t0 + lax.broadcasted_iota(jnp.int32, (n_sel, NSA_TQ), 1)) // SEL_BLOCK
    valid = blk <= cur
    forced = (blk == 0) | (valid & (blk > cur - SEL_FORCED_LOCAL))
    score = jnp.where(valid, imp_t + jnp.where(forced, FORCE_BONUS, 0.0), -1.0)
    rank = jnp.zeros((n_sel, NSA_TQ), jnp.int32)
    for mth in range(n_sel):
        other = score[mth:mth + 1, :]
        tie_first = jnp.where(blk > mth, 1, 0)
        rank = rank + jnp.where(other > score, 1, jnp.where(other == score, tie_first, 0))
    member_t = jnp.where(valid, jnp.where(rank < SEL_TOP_N, 1.0, 0.0), 0.0).astype(BF16)
    qrow = lax.broadcasted_iota(jnp.int32, (cols, NSA_TQ), 0) & (NSA_TQ - 1)
    eye = jnp.where(qrow == lax.broadcasted_iota(jnp.int32, (cols, NSA_TQ), 1), 1.0, 0.0).astype(BF16)
    member_pad = jnp.concatenate([member_t, jnp.ones((LANE - n_sel, NSA_TQ), BF16)], axis=0)
    picked = _nt_dot(eye, member_pad)
    lane = lax.broadcasted_iota(jnp.int32, (1, LANE), 1)
    slope_feat = []
    for h in range(NSA_HPG):
        row = jnp.zeros((1, LANE), F32)
        for part in range(2):
            for piece in range(SLOPE_PIECES):
                row = jnp.where(lane == n_sel + part * SLOPE_PIECES + piece,
                                slopes_ref[piece, g * NSA_HPG + h], row)
        slope_feat.append(jnp.broadcast_to(row, (NSA_TQ, LANE)))
    slope_feat = jnp.concatenate(slope_feat, axis=0)
    q_sel = jnp.concatenate([q, (jnp.where(picked > 0.5, 0.0, NEG_INF) + slope_feat).astype(BF16)], axis=1)
    q_win = jnp.concatenate([q, slope_feat.astype(BF16)], axis=1)

    def keys_aug(k_ref, start, size):
        return jnp.concatenate([k_ref[0, pl.ds(start, size), :], kf_ref[pl.ds(start, size), :]], axis=1)

    def sel_step(j, carry, causal):
        m_i, l_i, acc = carry
        start = pl.multiple_of(j * NSA_TK, NSA_TK)
        s_ = _nt_dot(keys_aug(ks_ref, start, NSA_TK), q_sel)
        if causal:
            pos = start + lax.broadcasted_iota(jnp.int32, (NSA_TK, 1), 0)
            s_ = jnp.where(pos <= t_row, s_, NEG_INF)
        m_new = jnp.maximum(m_i, jnp.max(s_, axis=0, keepdims=True))
        alpha = jnp.exp(m_i - m_new)
        p_ = jnp.exp(s_ - m_new)
        l_new = alpha * l_i + jnp.sum(p_, axis=0, keepdims=True)
        acc_new = alpha * acc + jnp.dot(vst_ref[0, 0, :, pl.ds(start, NSA_TK)], p_.astype(BF16),
                                        preferred_element_type=F32)
        return m_new, l_new, acc_new

    init = (jnp.full((1, cols), NEG_INF, F32), jnp.zeros((1, cols), F32),
            jnp.zeros((HEAD_DIM, cols), F32))
    last = (t0 + NSA_TQ - 1) // NSA_TK
    carry = lax.fori_loop(0, last, functools.partial(sel_step, causal=False), init)
    _, l_s, acc_s = sel_step(last, carry, causal=True)
    o_sel = acc_s / l_s

    def window(wstart, edge_only):
        pos_w = wstart + lax.broadcasted_iota(jnp.int32, (WIN_SPAN, 1), 0)
        s_w = _nt_dot(keys_aug(kw_ref, wstart, WIN_SPAN), q_win)
        rel_w = t_row - pos_w
        if edge_only:
            head = jnp.where(rel_w[:NSA_TQ] < WIN_SIZE, s_w[:NSA_TQ], NEG_INF)
            tail = jnp.where(rel_w[WIN_SIZE:] >= 0, s_w[WIN_SIZE:], NEG_INF)
            s_w = jnp.concatenate([head, s_w[NSA_TQ:WIN_SIZE], tail], axis=0)
        else:
            s_w = jnp.where(rel_w >= 0, jnp.where(rel_w < WIN_SIZE, s_w, NEG_INF), NEG_INF)
        p_w = jnp.exp(s_w - jnp.max(s_w, axis=0, keepdims=True))
        return (jnp.dot(vwt_ref[0, 0, :, pl.ds(wstart, WIN_SPAN)], p_w.astype(BF16),
                        preferred_element_type=F32) / jnp.sum(p_w, axis=0, keepdims=True))

    o_win = lax.cond(t0 >= WIN_SIZE,
                     lambda: window(pl.multiple_of(t0 - WIN_SIZE, NSA_TQ), True),
                     lambda: window(0, False))

    gates_t = jax.nn.sigmoid(gate_ref[0]).T
    for h in range(NSA_HPG):
        cs = slice(h * NSA_TQ, (h + 1) * NSA_TQ)
        gsel = [gates_t[h * N_GATES + k:h * N_GATES + k + 1, :] for k in range(N_GATES)]
        mix_t = gsel[0] * o_cmp[:, cs] + gsel[1] * o_sel[:, cs] + gsel[2] * o_win[:, cs]
        o_ref[0, :, h * HEAD_DIM:(h + 1) * HEAD_DIM] = mix_t.T


def _key_features(S):
    pos = jnp.arange(S)[:, None]
    lane = jnp.arange(LANE)[None, :]
    n_sel = S // SEL_BLOCK
    feat = jnp.where(lane == pos // SEL_BLOCK, 1, 0)
    feat = jnp.where((lane >= n_sel) & (lane < n_sel + SLOPE_PIECES), pos % SEL_BLOCK, feat)
    feat = jnp.where((lane >= n_sel + SLOPE_PIECES) & (lane < n_sel + 2 * SLOPE_PIECES),
                     (pos // SEL_BLOCK) * SEL_BLOCK, feat)
    return feat.astype(BF16)


def _slope_table(slopes):
    rows, rest = [], slopes
    for _ in range(SLOPE_PIECES):
        piece = rest.astype(BF16).astype(F32)
        rows.append(piece)
        rest = rest - piece
    return jnp.stack(rows + [slopes])


def _nsa_attention(qkv_b, kc, vc_t, vs_t, vw_t, gates, slopes_b, overlap_t, B, S):
    assert S % NSA_TK == 0 and S >= WIN_SPAN
    assert S // SEL_BLOCK <= SEL_BLOCK and S // SEL_BLOCK + 2 * SLOPE_PIECES <= LANE
    b3 = qkv_b.reshape(B, S, COLS_B)
    g3 = gates.reshape(B, S, GATE_PAD)
    n_cmp_pad = kc.shape[2]
    n_sel = S // SEL_BLOCK
    q_w = NSA_HPG * HEAD_DIM
    kv0 = WIDTH_B // HEAD_DIM
    per = KV_WIDTH_B // HEAD_DIM
    seq_blk = (1, S, HEAD_DIM)
    seq_t_blk = (1, 1, HEAD_DIM, S)
    grp = lambda b, g, i: (b, g, 0, 0)
    out = pl.pallas_call(
        functools.partial(_nsa_kernel, n_sel=n_sel),
        grid=(B, NSA_KV_GROUPS, S // NSA_TQ),
        in_specs=[
            pl.BlockSpec(memory_space=pltpu.SMEM),
            pl.BlockSpec((1, NSA_TQ, q_w), lambda b, g, i: (b, i, g)),
            pl.BlockSpec((1, 1, n_cmp_pad, HEAD_DIM), grp),
            pl.BlockSpec((1, 1, HEAD_DIM, n_cmp_pad), grp),
            pl.BlockSpec(seq_blk, lambda b, g, i: (b, 0, kv0 + 2 * per + g)),
            pl.BlockSpec(seq_t_blk, grp),
            pl.BlockSpec(seq_blk, lambda b, g, i: (b, 0, kv0 + 4 * per + g)),
            pl.BlockSpec(seq_t_blk, grp),
            pl.BlockSpec((S, LANE), lambda b, g, i: (0, 0)),
            pl.BlockSpec((1, NSA_TQ, LANE), lambda b, g, i: (b, i, g)),
            pl.BlockSpec((n_sel, n_cmp_pad), lambda b, g, i: (0, 0)),
        ],
        out_specs=pl.BlockSpec((1, NSA_TQ, q_w), lambda b, g, i: (b, i, g)),
        out_shape=jax.ShapeDtypeStruct((B, S, WIDTH_B), F32),
        compiler_params=_cparams(("arbitrary", "arbitrary", "arbitrary")),
        name="nsa_attention",
    )(_slope_table(slopes_b), b3, kc, vc_t, b3, vs_t, b3, vw_t, _key_features(S), g3, overlap_t)
    return out.reshape(B * S, WIDTH_B)


OUT_TM = 256


def _rms(x, g):
    return x * lax.rsqrt(jnp.mean(x * x, axis=-1, keepdims=True) + RMS_EPS) * g


def _outproj_kernel(o1_ref, o2_ref, o3_ref, l1_ref, l2_ref, l3_ref, ob_ref, x_ref, ga_ref, gb_ref,
                    w_ref, g2_ref, h_ref, hn_ref):
    l1, l2, l3 = l1_ref[...], l2_ref[...], l3_ref[...]
    mx = jnp.maximum(jnp.maximum(l1, l2), l3)
    e1, e2, e3 = jnp.exp(l1 - mx), jnp.exp(l2 - mx), jnp.exp(l3 - mx)
    den = e1 + e2 + e3
    w1, w2, w3 = e1 / den, e2 / den, e3 / den
    parts = []
    for h in range(N_HEADS_A):
        hs = slice(h * HEAD_DIM, (h + 1) * HEAD_DIM)
        parts.append(w1[:, h:h + 1] * o1_ref[:, hs] + w2[:, h:h + 1] * o2_ref[:, hs]
                     + w3[:, h:h + 1] * o3_ref[:, hs])
    o_a = jnp.concatenate(parts, axis=-1)
    mixed = jnp.concatenate([_rms(o_a, ga_ref[...]), _rms(ob_ref[...], gb_ref[...])], axis=-1)
    h = x_ref[...] + jnp.dot(mixed.astype(BF16), w_ref[...], preferred_element_type=F32)
    h_ref[...] = h
    hn_ref[...] = _rms(h, g2_ref[...]).astype(BF16)


def _out_projection(o_as, lses, o_b, x2, ga, gb, w_out, g2):
    T = x2.shape[0]
    row = lambda w: pl.BlockSpec((OUT_TM, w), lambda i: (i, 0))
    full = lambda r, c: pl.BlockSpec((r, c), lambda i: (0, 0))
    return pl.pallas_call(
        _outproj_kernel,
        grid=(T // OUT_TM,),
        in_specs=[row(WIDTH_A)] * 3 + [row(LANE)] * 3 + [row(WIDTH_B), row(D_MODEL),
                  full(1, WIDTH_A), full(1, WIDTH_B), full(WIDTH_A + WIDTH_B, D_MODEL), full(1, D_MODEL)],
        out_specs=[row(D_MODEL), row(D_MODEL)],
        out_shape=[jax.ShapeDtypeStruct((T, D_MODEL), F32), jax.ShapeDtypeStruct((T, D_MODEL), BF16)],
        compiler_params=_cparams(("arbitrary",)),
        name="out_projection",
    )(*o_as, *lses, o_b, x2, ga, gb, w_out, g2)


FFN_TM = 512
FFN_TH = 512


def _ffn_kernel(hn_ref, h_ref, wg_ref, wu_ref, wd_ref, gf_ref, o_ref, acc_ref):
    j = pl.program_id(1)
    hn = hn_ref[...]
    gate = jnp.dot(hn, wg_ref[...], preferred_element_type=F32)
    up = jnp.dot(hn, wu_ref[...], preferred_element_type=F32)
    act = (gate * jax.nn.sigmoid(gate) * up).astype(BF16)
    part = jnp.dot(act, wd_ref[...], preferred_element_type=F32)

    @pl.when(j == 0)
    def _():
        acc_ref[...] = part

    @pl.when(j > 0)
    def _():
        acc_ref[...] += part

    @pl.when(j == pl.num_programs(1) - 1)
    def _():
        o_ref[...] = _rms(h_ref[...] + acc_ref[...], gf_ref[...])


def _ffn(hn2, h, w_gate, w_up, w_down, gf):
    T = h.shape[0]
    return pl.pallas_call(
        _ffn_kernel,
        grid=(T // FFN_TM, FFN_HIDDEN // FFN_TH),
        in_specs=[
            pl.BlockSpec((FFN_TM, D_MODEL), lambda i, j: (i, 0)),
            pl.BlockSpec((FFN_TM, D_MODEL), lambda i, j: (i, 0)),
            pl.BlockSpec((D_MODEL, FFN_TH), lambda i, j: (0, j)),
            pl.BlockSpec((D_MODEL, FFN_TH), lambda i, j: (0, j)),
            pl.BlockSpec((FFN_TH, D_MODEL), lambda i, j: (j, 0)),
            pl.BlockSpec((1, D_MODEL), lambda i, j: (0, 0)),
        ],
        out_specs=pl.BlockSpec((FFN_TM, D_MODEL), lambda i, j: (i, 0)),
        out_shape=jax.ShapeDtypeStruct((T, D_MODEL), F32),
        scratch_shapes=[pltpu.VMEM((FFN_TM, D_MODEL), F32)],
        compiler_params=_cparams(("arbitrary", "arbitrary")),
        name="swiglu_ffn",
    )(hn2, h, w_gate, w_up, w_down, gf)


def _chunk_view(qkv_b3, col0):
    B, S, _ = qkv_b3.shape
    a = qkv_b3[:, :, col0:col0 + KV_WIDTH_B]
    a = a.reshape(B, S // CMP_STRIDE, CMP_STRIDE, NSA_KV_GROUPS, HEAD_DIM)
    return a.transpose(0, 3, 1, 2, 4).reshape(B, NSA_KV_GROUPS, S // CMP_STRIDE, CHUNK_FLAT)


def _overlap_t(n_sel, n_cmp_pad):
    cstart = jnp.arange(n_cmp_pad)[None, :] * CMP_STRIDE
    sstart = jnp.arange(n_sel)[:, None] * SEL_BLOCK
    return ((cstart < sstart + SEL_BLOCK) & (cstart + CMP_BLOCK > sstart)).astype(BF16)


def kernel(x, norm1_g, w_in, cmp_pe_k, cmp_w1_k, cmp_w2_k, cmp_pe_v, cmp_w1_v, cmp_w2_v, grp_norm_a,
           grp_norm_b, w_out, norm2_g, w_gate, w_up, w_down, final_g):
    B, S, D = x.shape
    assert D == D_MODEL and S % (DIL_BLOCK * max(d for _, d in DIL_CONFIGS)) == 0
    assert (B * S) % IN_TM == 0 and w_in.shape[0] == 1, "single-layer block only"
    T = B * S
    slopes = jnp.exp2(-8.0 * jnp.arange(1, N_HEADS + 1, dtype=F32) / N_HEADS)
    slopes_a, slopes_b = slopes[0::2], slopes[1::2]
    n_main = COLS_A + COLS_B
    per_group = NSA_HPG * N_GATES

    x2 = x.reshape(T, D)
    w_main = w_in[0][:, :n_main].astype(BF16)
    w_g = w_in[0][:, n_main:].reshape(D, NSA_KV_GROUPS, per_group)
    w_g = jnp.pad(w_g, ((0, 0), (0, 0), (0, LANE - per_group))).reshape(D, GATE_PAD).astype(BF16)
    qkv_a, qkv_b, gates = _in_projection(x2, norm1_g[0][None, :], w_main, w_g)

    dil = [_dilated_attention(qkv_a, slopes_a, B, S, w, d) for w, d in DIL_CONFIGS]

    b3 = qkv_b.reshape(B, S, COLS_B)
    pe2 = lambda pe: pe.reshape(2, CHUNK_FLAT)
    kc = _compress(_chunk_view(b3, WIDTH_B), pe2(cmp_pe_k[0]), cmp_w1_k[0].astype(BF16),
                   cmp_w2_k[0].astype(BF16))
    vc = _compress(_chunk_view(b3, WIDTH_B + KV_WIDTH_B), pe2(cmp_pe_v[0]), cmp_w1_v[0].astype(BF16),
                   cmp_w2_v[0].astype(BF16))
    col_vs = WIDTH_B + 3 * KV_WIDTH_B
    col_vw = WIDTH_B + 5 * KV_WIDTH_B
    seq_t = lambda c0: b3[:, :, c0:c0 + KV_WIDTH_B].reshape(B, S, NSA_KV_GROUPS, HEAD_DIM).transpose(0, 2, 3, 1)
    o_b = _nsa_attention(qkv_b, kc, vc.transpose(0, 1, 3, 2), seq_t(col_vs), seq_t(col_vw), gates, slopes_b,
                         _overlap_t(S // SEL_BLOCK, kc.shape[2]), B, S)

    h, hn2 = _out_projection([o for o, _ in dil], [s for _, s in dil], o_b, x2,
                             grp_norm_a[0][None, :], grp_norm_b[0][None, :],
                             w_out[0].astype(BF16), norm2_g[0][None, :])
    out = _ffn(hn2, h, w_gate[0].astype(BF16), w_up[0].astype(BF16), w_down[0].astype(BF16),
               final_g[None, :])
    return out.reshape(B, S, D)
```

```python
import functools
import math

import jax
import jax.numpy as jnp
from jax import lax
from jax.experimental import pallas as pl
from jax.experimental.pallas import tpu as pltpu

F32 = jnp.float32
BF16 = jnp.bfloat16

D_MODEL = 2048
HEAD_DIM = 128
N_HEADS = 16
N_HEADS_A = 8
N_HEADS_B = 8
WIDTH_A = N_HEADS_A * HEAD_DIM
WIDTH_B = N_HEADS_B * HEAD_DIM
DIL_CONFIGS = ((128, 1), (512, 4), (2048, 16))
DIL_BLOCK = 128
NSA_KV_GROUPS = 2
NSA_HPG = N_HEADS_B // NSA_KV_GROUPS
KV_WIDTH_B = NSA_KV_GROUPS * HEAD_DIM
CMP_BLOCK = 32
CMP_STRIDE = 16
CMP_HIDDEN = 2 * HEAD_DIM
SEL_BLOCK = 64
SEL_TOP_N = 16
SEL_FORCED_LOCAL = 2
FORCE_BONUS = 1.0e3
WIN_SIZE = 512
N_GATES = 3
FFN_HIDDEN = 5632
RMS_EPS = 1e-6
NEG_INF = -1e30

COLS_A = 3 * WIDTH_A
COLS_B = WIDTH_B + 6 * KV_WIDTH_B
LANE = 128
GATE_PAD = NSA_KV_GROUPS * LANE

VMEM_LIMIT = 56 * 1024 * 1024


def _cparams(sem):
    return pltpu.CompilerParams(dimension_semantics=sem, vmem_limit_bytes=VMEM_LIMIT)


def _nt_dot(a, b):
    return lax.dot_general(a, b, (((1,), (1,)), ((), ())), preferred_element_type=F32)


IN_TM = 1024
IN_TN = 512
N_TILES_A = COLS_A // IN_TN
N_TILES_B = COLS_B // IN_TN
Q_B_TILE0 = N_TILES_A


def _inproj_kernel(x_ref, g_ref, w_ref, wg_ref, oa_ref, ob_ref, og_ref, xn_ref, *, scale):
    j = pl.program_id(1)

    @pl.when(j == 0)
    def _():
        x = x_ref[...]
        y = x * lax.rsqrt(jnp.mean(x * x, axis=-1, keepdims=True) + RMS_EPS)
        xn = (y * g_ref[...]).astype(BF16)
        xn_ref[...] = xn
        og_ref[...] = jnp.dot(xn, wg_ref[...], preferred_element_type=F32)

    acc = jnp.dot(xn_ref[...], w_ref[...], preferred_element_type=F32)
    is_q = (j < 2) | ((j >= Q_B_TILE0) & (j < Q_B_TILE0 + 2))
    acc = acc * jnp.where(is_q, scale, 1.0).astype(F32)

    @pl.when(j < N_TILES_A)
    def _():
        oa_ref[...] = acc.astype(BF16)

    @pl.when(j >= N_TILES_A)
    def _():
        ob_ref[...] = acc.astype(BF16)


def _in_projection(x2, g, w_main, w_gate):
    T = x2.shape[0]
    grid = (T // IN_TM, N_TILES_A + N_TILES_B)
    return pl.pallas_call(
        functools.partial(_inproj_kernel, scale=1.0 / math.sqrt(HEAD_DIM)),
        grid=grid,
        in_specs=[
            pl.BlockSpec((IN_TM, D_MODEL), lambda i, j: (i, 0)),
            pl.BlockSpec((1, D_MODEL), lambda i, j: (0, 0)),
            pl.BlockSpec((D_MODEL, IN_TN), lambda i, j: (0, j)),
            pl.BlockSpec((D_MODEL, GATE_PAD), lambda i, j: (0, 0)),
        ],
        out_specs=[
            pl.BlockSpec((IN_TM, IN_TN), lambda i, j: (i, jnp.minimum(j, N_TILES_A - 1))),
            pl.BlockSpec((IN_TM, IN_TN), lambda i, j: (i, jnp.maximum(j - N_TILES_A, 0))),
            pl.BlockSpec((IN_TM, GATE_PAD), lambda i, j: (i, 0)),
        ],
        out_shape=[
            jax.ShapeDtypeStruct((T, COLS_A), BF16),
            jax.ShapeDtypeStruct((T, COLS_B), BF16),
            jax.ShapeDtypeStruct((T, GATE_PAD), F32),
        ],
        scratch_shapes=[pltpu.VMEM((IN_TM, D_MODEL), BF16)],
        compiler_params=_cparams(("arbitrary", "arbitrary")),
        name="in_projection",
    )(x2, g, w_main, w_gate)


DIL_HEAD_GROUP = 4
DIL_COLS = DIL_HEAD_GROUP * HEAD_DIM
DIL_LSE_W = (N_HEADS_A // DIL_HEAD_GROUP) * LANE
DIL_PERM = 256


def _dilated_heads(q, k_prev, k_cur, v_prev, v_cur, slopes, has_prev, dilation, span):
    key = lax.broadcasted_iota(jnp.int32, (2 * DIL_BLOCK, DIL_BLOCK), 0)
    qry = lax.broadcasted_iota(jnp.int32, (2 * DIL_BLOCK, DIL_BLOCK), 1)
    rel = qry + DIL_BLOCK - key
    in_prev = jnp.where(rel <= span, jnp.where(has_prev, 0.0, NEG_INF), NEG_INF)
    mask_bias = jnp.where(key < DIL_BLOCK, in_prev, jnp.where(rel >= 0, 0.0, NEG_INF))
    dist = (rel * dilation).astype(F32)
    heads = [slice(h * HEAD_DIM, (h + 1) * HEAD_DIM) for h in range(len(slopes))]
    scores = [_nt_dot(jnp.concatenate([k_prev[:, hs], k_cur[:, hs]], axis=0), q[:, hs]) for hs in heads]
    scores = [s - slope * dist + mask_bias for slope, s in zip(slopes, scores)]
    maxes = [jnp.max(s, axis=0, keepdims=True) for s in scores]
    probs = [jnp.exp(s - m) for s, m in zip(scores, maxes)]
    sums = [jnp.sum(p, axis=0, keepdims=True) for p in probs]
    outs = [lax.dot_general((p * (1.0 / l)).astype(BF16),
                            jnp.concatenate([v_prev[:, hs], v_cur[:, hs]], axis=0),
                            (((0,), (0,)), ((), ())), preferred_element_type=F32)
            for hs, p, l in zip(heads, probs, sums)]
    lse_rows = jnp.concatenate([m + jnp.log(l) for m, l in zip(maxes, sums)]
                               + [jnp.zeros((LANE - len(slopes), DIL_BLOCK), F32)], axis=0)
    return jnp.concatenate(outs, axis=1), lse_rows.T


def _dilated_kernel(slopes_ref, q_ref, kc_ref, kp_ref, vc_ref, vp_ref, o_ref, lse_ref, *,
                    dilation, span):
    has_prev = pl.program_id(2) > 0
    for grp in range(N_HEADS_A // DIL_HEAD_GROUP):
        cs = slice(grp * DIL_COLS, (grp + 1) * DIL_COLS)
        slopes = [slopes_ref[grp * DIL_HEAD_GROUP + i] for i in range(DIL_HEAD_GROUP)]
        o, lse = _dilated_heads(q_ref[0, :, cs], kp_ref[0, :, cs], kc_ref[0, :, cs], vp_ref[0, :, cs],
                                vc_ref[0, :, cs], slopes, has_prev, dilation, span)
        o_ref[0, :, cs] = o
        lse_ref[0, :, grp * LANE:(grp + 1) * LANE] = lse


def _split_bf16(x, pieces):
    out, rest = [], x
    for _ in range(pieces):
        part = rest.astype(BF16)
        out.append(part)
        rest = rest - part.astype(F32)
    return out


def _strided_kernel(slopes_ref, perm_ref, perm_t_ref, q_ref, k_ref, v_ref, o_ref, lse_ref,
                    qp_ref, kp_ref, vp_ref, op_ref, lp_ref, *, dilation, span):
    grp = pl.program_id(1)
    n = pl.program_id(2)
    n_sub = q_ref.shape[1] // DIL_PERM
    width = DIL_PERM // dilation
    cur = n & 1
    prev = 1 - cur
    perm = perm_ref[...]

    def deinterleave(x):
        y = jnp.dot(perm, x, preferred_element_type=F32).astype(BF16)
        return y.reshape(dilation, width, x.shape[1])

    for sub in range(n_sub):
        rows = slice(sub * DIL_PERM, (sub + 1) * DIL_PERM)
        qp_ref[:, sub] = deinterleave(q_ref[0, rows, :])
        kp_ref[cur, :, sub] = deinterleave(k_ref[0, rows, :])
        vp_ref[cur, :, sub] = deinterleave(v_ref[0, rows, :])

    @pl.when(n == 0)
    def _():
        kp_ref[prev] = jnp.zeros(kp_ref.shape[1:], BF16)
        vp_ref[prev] = jnp.zeros(vp_ref.shape[1:], BF16)

    slopes = [slopes_ref[grp * DIL_HEAD_GROUP + i] for i in range(DIL_HEAD_GROUP)]
    blk_rows = lambda a: a.reshape(DIL_BLOCK, a.shape[-1])

    def residue(r, carry):
        o, lse = _dilated_heads(blk_rows(qp_ref[r]), blk_rows(kp_ref[prev, r]), blk_rows(kp_ref[cur, r]),
                                blk_rows(vp_ref[prev, r]), blk_rows(vp_ref[cur, r]), slopes, n > 0,
                                dilation, span)
        op_ref[r] = o.reshape(n_sub, width, DIL_COLS)
        lp_ref[r] = lse.reshape(n_sub, width, LANE)
        return carry

    lax.fori_loop(0, dilation, residue, 0)

    perm_t = perm_t_ref[...]
    for sub in range(n_sub):
        rows = slice(sub * DIL_PERM, (sub + 1) * DIL_PERM)
        o_sub = op_ref[:, sub].reshape(DIL_PERM, DIL_COLS)
        o_ref[0, rows, :] = sum(jnp.dot(perm_t, part, preferred_element_type=F32)
                                for part in _split_bf16(o_sub, 2))
        l_sub = lp_ref[:, sub].reshape(DIL_PERM, LANE)
        lse_ref[0, rows, :] = sum(jnp.dot(perm_t, part, preferred_element_type=F32)
                                  for part in _split_bf16(l_sub, 3))


def _contiguous_attention(qkv_a, slopes_a, B, S, window):
    nb = S // DIL_BLOCK
    a3 = qkv_a.reshape(B, S, COLS_A)
    blk = (1, DIL_BLOCK, WIDTH_A)
    prev = lambda n: jnp.maximum(n - 1, 0)
    o, lse = pl.pallas_call(
        functools.partial(_dilated_kernel, dilation=1, span=window),
        grid=(B, 1, nb),
        in_specs=[
            pl.BlockSpec(memory_space=pltpu.SMEM),
            pl.BlockSpec(blk, lambda b, r, n: (b, n, 0)),
            pl.BlockSpec(blk, lambda b, r, n: (b, n, 1)),
            pl.BlockSpec(blk, lambda b, r, n: (b, prev(n), 1)),
            pl.BlockSpec(blk, lambda b, r, n: (b, n, 2)),
            pl.BlockSpec(blk, lambda b, r, n: (b, prev(n), 2)),
        ],
        out_specs=[
            pl.BlockSpec(blk, lambda b, r, n: (b, n, 0)),
            pl.BlockSpec((1, DIL_BLOCK, DIL_LSE_W), lambda b, r, n: (b, n, 0)),
        ],
        out_shape=[
            jax.ShapeDtypeStruct((B, S, WIDTH_A), F32),
            jax.ShapeDtypeStruct((B, S, DIL_LSE_W), F32),
        ],
        compiler_params=_cparams(("arbitrary", "arbitrary", "arbitrary")),
        name="dilated_attention_d1",
    )(slopes_a, a3, a3, a3, a3, a3)
    return o.reshape(B * S, WIDTH_A), lse.reshape(B * S, DIL_LSE_W)


def _permutation(dilation):
    width = DIL_PERM // dilation
    out_row = jnp.arange(DIL_PERM)
    src = (out_row % width) * dilation + out_row // width
    return (src[:, None] == jnp.arange(DIL_PERM)[None, :]).astype(BF16)


def _strided_attention(qkv_a, slopes_a, B, S, window, dilation):
    rows = DIL_BLOCK * dilation
    assert S % rows == 0 and rows % DIL_PERM == 0 and DIL_PERM % dilation == 0
    assert (DIL_PERM // dilation) % 16 == 0
    n_sub, width = rows // DIL_PERM, DIL_PERM // dilation
    n_grp = N_HEADS_A // DIL_HEAD_GROUP
    per = WIDTH_A // DIL_COLS
    a3 = qkv_a.reshape(B, S, COLS_A)
    perm = _permutation(dilation)
    blk = (1, rows, DIL_COLS)
    whole = lambda b, g, n: (0, 0)
    packed = lambda dt: pltpu.VMEM((dilation, n_sub, width, DIL_COLS), dt)
    o, lse = pl.pallas_call(
        functools.partial(_strided_kernel, dilation=dilation, span=window // dilation),
        grid=(B, n_grp, S // rows),
        in_specs=[
            pl.BlockSpec(memory_space=pltpu.SMEM),
            pl.BlockSpec((DIL_PERM, DIL_PERM), whole),
            pl.BlockSpec((DIL_PERM, DIL_PERM), whole),
            pl.BlockSpec(blk, lambda b, g, n: (b, n, g)),
            pl.BlockSpec(blk, lambda b, g, n: (b, n, per + g)),
            pl.BlockSpec(blk, lambda b, g, n: (b, n, 2 * per + g)),
        ],
        out_specs=[
            pl.BlockSpec(blk, lambda b, g, n: (b, n, g)),
            pl.BlockSpec((1, rows, LANE), lambda b, g, n: (b, n, g)),
        ],
        out_shape=[
            jax.ShapeDtypeStruct((B, S, WIDTH_A), F32),
            jax.ShapeDtypeStruct((B, S, DIL_LSE_W), F32),
        ],
        scratch_shapes=[
            packed(BF16),
            pltpu.VMEM((2, dilation, n_sub, width, DIL_COLS), BF16),
            pltpu.VMEM((2, dilation, n_sub, width, DIL_COLS), BF16),
            packed(F32),
            pltpu.VMEM((dilation, n_sub, width, LANE), F32),
        ],
        compiler_params=_cparams(("arbitrary", "arbitrary", "arbitrary")),
        name=f"dilated_attention_d{dilation}",
    )(slopes_a, perm, perm.T, a3, a3, a3)
    return o.reshape(B * S, WIDTH_A), lse.reshape(B * S, DIL_LSE_W)


CHUNK_FLAT = CMP_STRIDE * HEAD_DIM


def _compress_kernel(x_ref, pe_ref, w1_ref, w2_ref, o_ref):
    x = x_ref[0, 0].astype(F32)
    first = jnp.dot((x + pe_ref[0:1, :]).astype(BF16), w1_ref[0:CHUNK_FLAT, :],
                    preferred_element_type=F32)
    second = jnp.dot((x + pe_ref[1:2, :]).astype(BF16), w1_ref[CHUNK_FLAT:2 * CHUNK_FLAT, :],
                     preferred_element_type=F32)
    nc = x.shape[0]
    hidden = first + pltpu.roll(second, shift=nc - 1, axis=0)
    act = hidden * jax.nn.sigmoid(hidden)
    o_ref[0, 0] = jnp.dot(act.astype(BF16), w2_ref[...], preferred_element_type=F32).astype(BF16)


def _compress(chunks, pe2, w1, w2):
    B, G, nc, _ = chunks.shape
    return pl.pallas_call(
        _compress_kernel,
        grid=(B, G),
        in_specs=[
            pl.BlockSpec((1, 1, nc, CHUNK_FLAT), lambda b, g: (b, g, 0, 0)),
            pl.BlockSpec((2, CHUNK_FLAT), lambda b, g: (0, 0)),
            pl.BlockSpec((2 * CHUNK_FLAT, CMP_HIDDEN), lambda b, g: (0, 0)),
            pl.BlockSpec((CMP_HIDDEN, HEAD_DIM), lambda b, g: (0, 0)),
        ],
        out_specs=pl.BlockSpec((1, 1, nc, HEAD_DIM), lambda b, g: (b, g, 0, 0)),
        out_shape=jax.ShapeDtypeStruct((B, G, nc, HEAD_DIM), BF16),
        compiler_params=_cparams(("arbitrary", "arbitrary")),
        name="nsa_compress",
    )(chunks, pe2, w1, w2)


NSA_TQ = 256
NSA_TK = 512
WIN_SPAN = WIN_SIZE + NSA_TQ
SLOPE_PIECES = 3


def _stack_heads(a):
    return jnp.concatenate([a[:, h * HEAD_DIM:(h + 1) * HEAD_DIM] for h in range(NSA_HPG)], axis=0)


def _nsa_kernel(slopes_ref, q_ref, kc_ref, vct_ref, ks_ref, vst_ref, kw_ref, vwt_ref, kf_ref, gate_ref,
                ovl_ref, o_ref, *, n_sel):
    g = pl.program_id(1)
    qi = pl.program_id(2)
    t0 = qi * NSA_TQ
    cols = NSA_HPG * NSA_TQ

    q = _stack_heads(q_ref[0])
    slope_row = jnp.concatenate(
        [jnp.full((1, NSA_TQ), slopes_ref[SLOPE_PIECES, g * NSA_HPG + h], F32) for h in range(NSA_HPG)],
        axis=1)
    t_row = t0 + (lax.broadcasted_iota(jnp.int32, (1, cols), 1) & (NSA_TQ - 1))

    n_cmp_pad = kc_ref.shape[2]
    cend = (lax.broadcasted_iota(jnp.int32, (n_cmp_pad, 1), 0) * CMP_STRIDE + (CMP_BLOCK - 1))
    rel_c = t_row - cend
    ok_c = rel_c >= 0
    s = _nt_dot(kc_ref[0, 0], q)
    s = s - slope_row * rel_c.astype(F32)
    s = jnp.where(ok_c, s, NEG_INF)
    m = jnp.max(s, axis=0, keepdims=True)
    e = jnp.where(ok_c, jnp.exp(s - m), 0.0)
    p = e / jnp.maximum(jnp.sum(e, axis=0, keepdims=True), 1e-30)
    o_cmp = jnp.dot(vct_ref[0, 0], p.astype(BF16), preferred_element_type=F32)

    p_grp = p[:, 0:NSA_TQ]
    for h in range(1, NSA_HPG):
        p_grp = p_grp + p[:, h * NSA_TQ:(h + 1) * NSA_TQ]
    p_hi = p_grp.astype(BF16)
    p_lo = (p_grp - p_hi.astype(F32)).astype(BF16)
    ovl = ovl_ref[...]
    imp_t = (jnp.dot(ovl, p_hi, preferred_element_type=F32)
             + jnp.dot(ovl, p_lo, preferred_element_type=F32))
    blk = lax.broadcasted_iota(jnp.int32, (n_sel, NSA_TQ), 0)
    cur = (t0 + lax.broadcasted_iota(jnp.int32, (n_sel, NSA_TQ), 1)) // SEL_BLOCK
    valid = blk <= cur
    forced = (blk == 0) | (valid & (blk > cur - SEL_FORCED_LOCAL))
    score = jnp.where(valid, imp_t + jnp.where(forced, FORCE_BONUS, 0.0), -1.0)
    rank = jnp.zeros((n_sel, NSA_TQ), jnp.int32)
    for mth in range(n_sel):
        other = score[mth:mth + 1, :]
        tie_first = jnp.where(blk > mth, 1, 0)
        rank = rank + jnp.where(other > score, 1, jnp.where(other == score, tie_first, 0))
    member_t = jnp.where(valid, jnp.where(rank < SEL_TOP_N, 1.0, 0.0), 0.0).astype(BF16)
    qrow = lax.broadcasted_iota(jnp.int32, (cols, NSA_TQ), 0) & (NSA_TQ - 1)
    eye = jnp.where(qrow == lax.broadcasted_iota(jnp.int32, (cols, NSA_TQ), 1), 1.0, 0.0).astype(BF16)
    member_pad = jnp.concatenate([member_t, jnp.ones((LANE - n_sel, NSA_TQ), BF16)], axis=0)
    picked = _nt_dot(eye, member_pad)
    lane = lax.broadcasted_iota(jnp.int32, (1, LANE), 1)
    slope_feat = []
    for h in range(NSA_HPG):
        row = jnp.zeros((1, LANE), F32)
        for part in range(2):
            for piece in range(SLOPE_PIECES):
                row = jnp.where(lane == n_sel + part * SLOPE_PIECES + piece,
                                slopes_ref[piece, g * NSA_HPG + h], row)
        slope_feat.append(jnp.broadcast_to(row, (NSA_TQ, LANE)))
    slope_feat = jnp.concatenate(slope_feat, axis=0)
    q_sel = jnp.concatenate([q, (jnp.where(picked > 0.5, 0.0, NEG_INF) + slope_feat).astype(BF16)], axis=1)
    q_win = jnp.concatenate([q, slope_feat.astype(BF16)], axis=1)

    def keys_aug(k_ref, start, size):
        return jnp.concatenate([k_ref[0, pl.ds(start, size), :], kf_ref[pl.ds(start, size), :]], axis=1)

    def sel_step(j, carry, causal):
        m_i, l_i, acc = carry
        start = pl.multiple_of(j * NSA_TK, NSA_TK)
        s_ = _nt_dot(keys_aug(ks_ref, start, NSA_TK), q_sel)
        if causal:
            pos = start + lax.broadcasted_iota(jnp.int32, (NSA_TK, 1), 0)
            s_ = jnp.where(pos <= t_row, s_, NEG_INF)
        m_new = jnp.maximum(m_i, jnp.max(s_, axis=0, keepdims=True))
        alpha = jnp.exp(m_i - m_new)
        p_ = jnp.exp(s_ - m_new)
        l_new = alpha * l_i + jnp.sum(p_, axis=0, keepdims=True)
        acc_new = alpha * acc + jnp.dot(vst_ref[0, 0, :, pl.ds(start, NSA_TK)], p_.astype(BF16),
                                        preferred_element_type=F32)
        return m_new, l_new, acc_new

    init = (jnp.full((1, cols), NEG_INF, F32), jnp.zeros((1, cols), F32),
            jnp.zeros((HEAD_DIM, cols), F32))
    last = (t0 + NSA_TQ - 1) // NSA_TK
    carry = lax.fori_loop(0, last, functools.partial(sel_step, causal=False), init)
    _, l_s, acc_s = sel_step(last, carry, causal=True)
    o_sel = acc_s / l_s

    def window(wstart, edge_only):
        pos_w = wstart + lax.broadcasted_iota(jnp.int32, (WIN_SPAN, 1), 0)
        s_w = _nt_dot(keys_aug(kw_ref, wstart, WIN_SPAN), q_win)
        rel_w = t_row - pos_w
        if edge_only:
            head = jnp.where(rel_w[:NSA_TQ] < WIN_SIZE, s_w[:NSA_TQ], NEG_INF)
            tail = jnp.where(rel_w[WIN_SIZE:] >= 0, s_w[WIN_SIZE:], NEG_INF)
            s_w = jnp.concatenate([head, s_w[NSA_TQ:WIN_SIZE], tail], axis=0)
        else:
            s_w = jnp.where(rel_w >= 0, jnp.where(rel_w < WIN_SIZE, s_w, NEG_INF), NEG_INF)
        p_w = jnp.exp(s_w - jnp.max(s_w, axis=0, keepdims=True))
        return (jnp.dot(vwt_ref[0, 0, :, pl.ds(wstart, WIN_SPAN)], p_w.astype(BF16),
                        preferred_element_type=F32) / jnp.sum(p_w, axis=0, keepdims=True))

    o_win = lax.cond(t0 >= WIN_SIZE,
                     lambda: window(pl.multiple_of(t0 - WIN_SIZE, NSA_TQ), True),
                     lambda: window(0, False))

    gates_t = jax.nn.sigmoid(gate_ref[0]).T
    for h in range(NSA_HPG):
        cs = slice(h * NSA_TQ, (h + 1) * NSA_TQ)
        gsel = [gates_t[h * N_GATES + k:h * N_GATES + k + 1, :] for k in range(N_GATES)]
        mix_t = gsel[0] * o_cmp[:, cs] + gsel[1] * o_sel[:, cs] + gsel[2] * o_win[:, cs]
        o_ref[0, :, h * HEAD_DIM:(h + 1) * HEAD_DIM] = mix_t.T


def _key_features(S):
    pos = jnp.arange(S)[:, None]
    lane = jnp.arange(LANE)[None, :]
    n_sel = S // SEL_BLOCK
    feat = jnp.where(lane == pos // SEL_BLOCK, 1, 0)
    feat = jnp.where((lane >= n_sel) & (lane < n_sel + SLOPE_PIECES), pos % SEL_BLOCK, feat)
    feat = jnp.where((lane >= n_sel + SLOPE_PIECES) & (lane < n_sel + 2 * SLOPE_PIECES),
                     (pos // SEL_BLOCK) * SEL_BLOCK, feat)
    return feat.astype(BF16)


def _slope_table(slopes):
    rows, rest = [], slopes
    for _ in range(SLOPE_PIECES):
        piece = rest.astype(BF16).astype(F32)
        rows.append(piece)
        rest = rest - piece
    return jnp.stack(rows + [slopes])


def _nsa_attention(qkv_b, kc, vc_t, vs_t, vw_t, gates, slopes_b, overlap_t, B, S):
    assert S % NSA_TK == 0 and S >= WIN_SPAN
    assert S // SEL_BLOCK <= SEL_BLOCK and S // SEL_BLOCK + 2 * SLOPE_PIECES <= LANE
    b3 = qkv_b.reshape(B, S, COLS_B)
    g3 = gates.reshape(B, S, GATE_PAD)
    n_cmp_pad = kc.shape[2]
    n_sel = S // SEL_BLOCK
    q_w = NSA_HPG * HEAD_DIM
    kv0 = WIDTH_B // HEAD_DIM
    per = KV_WIDTH_B // HEAD_DIM
    seq_blk = (1, S, HEAD_DIM)
    seq_t_blk = (1, 1, HEAD_DIM, S)
    grp = lambda b, g, i: (b, g, 0, 0)
    out = pl.pallas_call(
        functools.partial(_nsa_kernel, n_sel=n_sel),
        grid=(B, NSA_KV_GROUPS, S // NSA_TQ),
        in_specs=[
            pl.BlockSpec(memory_space=pltpu.SMEM),
            pl.BlockSpec((1, NSA_TQ, q_w), lambda b, g, i: (b, i, g)),
            pl.BlockSpec((1, 1, n_cmp_pad, HEAD_DIM), grp),
            pl.BlockSpec((1, 1, HEAD_DIM, n_cmp_pad), grp),
            pl.BlockSpec(seq_blk, lambda b, g, i: (b, 0, kv0 + 2 * per + g)),
            pl.BlockSpec(seq_t_blk, grp),
            pl.BlockSpec(seq_blk, lambda b, g, i: (b, 0, kv0 + 4 * per + g)),
            pl.BlockSpec(seq_t_blk, grp),
            pl.BlockSpec((S, LANE), lambda b, g, i: (0, 0)),
            pl.BlockSpec((1, NSA_TQ, LANE), lambda b, g, i: (b, i, g)),
            pl.BlockSpec((n_sel, n_cmp_pad), lambda b, g, i: (0, 0)),
        ],
        out_specs=pl.BlockSpec((1, NSA_TQ, q_w), lambda b, g, i: (b, i, g)),
        out_shape=jax.ShapeDtypeStruct((B, S, WIDTH_B), F32),
        compiler_params=_cparams(("arbitrary", "arbitrary", "arbitrary")),
        name="nsa_attention",
    )(_slope_table(slopes_b), b3, kc, vc_t, b3, vs_t, b3, vw_t, _key_features(S), g3, overlap_t)
    return out.reshape(B * S, WIDTH_B)


OUT_TM = 256


def _rms(x, g):
    return x * lax.rsqrt(jnp.mean(x * x, axis=-1, keepdims=True) + RMS_EPS) * g


def _outproj_kernel(o1_ref, o2_ref, o3_ref, l1_ref, l2_ref, l3_ref, ob_ref, x_ref, ga_ref, gb_ref,
                    w_ref, g2_ref, h_ref, hn_ref):
    l1, l2, l3 = l1_ref[...], l2_ref[...], l3_ref[...]
    mx = jnp.maximum(jnp.maximum(l1, l2), l3)
    e1, e2, e3 = jnp.exp(l1 - mx), jnp.exp(l2 - mx), jnp.exp(l3 - mx)
    den = e1 + e2 + e3
    w1, w2, w3 = e1 / den, e2 / den, e3 / den
    parts = []
    for h in range(N_HEADS_A):
        hs = slice(h * HEAD_DIM, (h + 1) * HEAD_DIM)
        c = (h // DIL_HEAD_GROUP) * LANE + h % DIL_HEAD_GROUP
        parts.append(w1[:, c:c + 1] * o1_ref[:, hs] + w2[:, c:c + 1] * o2_ref[:, hs]
                     + w3[:, c:c + 1] * o3_ref[:, hs])
    o_a = jnp.concatenate(parts, axis=-1)
    mixed = jnp.concatenate([_rms(o_a, ga_ref[...]), _rms(ob_ref[...], gb_ref[...])], axis=-1)
    h = x_ref[...] + jnp.dot(mixed.astype(BF16), w_ref[...], preferred_element_type=F32)
    h_ref[...] = h
    hn_ref[...] = _rms(h, g2_ref[...]).astype(BF16)


def _out_projection(o_as, lses, o_b, x2, ga, gb, w_out, g2):
    T = x2.shape[0]
    row = lambda w: pl.BlockSpec((OUT_TM, w), lambda i: (i, 0))
    full = lambda r, c: pl.BlockSpec((r, c), lambda i: (0, 0))
    return pl.pallas_call(
        _outproj_kernel,
        grid=(T // OUT_TM,),
        in_specs=[row(WIDTH_A)] * 3 + [row(DIL_LSE_W)] * 3 + [row(WIDTH_B), row(D_MODEL),
                  full(1, WIDTH_A), full(1, WIDTH_B), full(WIDTH_A + WIDTH_B, D_MODEL), full(1, D_MODEL)],
        out_specs=[row(D_MODEL), row(D_MODEL)],
        out_shape=[jax.ShapeDtypeStruct((T, D_MODEL), F32), jax.ShapeDtypeStruct((T, D_MODEL), BF16)],
        compiler_params=_cparams(("arbitrary",)),
        name="out_projection",
    )(*o_as, *lses, o_b, x2, ga, gb, w_out, g2)


FFN_TM = 512
FFN_TH = 512


def _ffn_kernel(hn_ref, h_ref, wg_ref, wu_ref, wd_ref, gf_ref, o_ref, acc_ref):
    j = pl.program_id(1)
    hn = hn_ref[...]
    gate = jnp.dot(hn, wg_ref[...], preferred_element_type=F32)
    up = jnp.dot(hn, wu_ref[...], preferred_element_type=F32)
    act = (gate * jax.nn.sigmoid(gate) * up).astype(BF16)
    part = jnp.dot(act, wd_ref[...], preferred_element_type=F32)

    @pl.when(j == 0)
    def _():
        acc_ref[...] = part

    @pl.when(j > 0)
    def _():
        acc_ref[...] += part

    @pl.when(j == pl.num_programs(1) - 1)
    def _():
        o_ref[...] = _rms(h_ref[...] + acc_ref[...], gf_ref[...])


def _ffn(hn2, h, w_gate, w_up, w_down, gf):
    T = h.shape[0]
    return pl.pallas_call(
        _ffn_kernel,
        grid=(T // FFN_TM, FFN_HIDDEN // FFN_TH),
        in_specs=[
            pl.BlockSpec((FFN_TM, D_MODEL), lambda i, j: (i, 0)),
            pl.BlockSpec((FFN_TM, D_MODEL), lambda i, j: (i, 0)),
            pl.BlockSpec((D_MODEL, FFN_TH), lambda i, j: (0, j)),
            pl.BlockSpec((D_MODEL, FFN_TH), lambda i, j: (0, j)),
            pl.BlockSpec((FFN_TH, D_MODEL), lambda i, j: (j, 0)),
            pl.BlockSpec((1, D_MODEL), lambda i, j: (0, 0)),
        ],
        out_specs=pl.BlockSpec((FFN_TM, D_MODEL), lambda i, j: (i, 0)),
        out_shape=jax.ShapeDtypeStruct((T, D_MODEL), F32),
        scratch_shapes=[pltpu.VMEM((FFN_TM, D_MODEL), F32)],
        compiler_params=_cparams(("arbitrary", "arbitrary")),
        name="swiglu_ffn",
    )(hn2, h, w_gate, w_up, w_down, gf)


def _chunk_view(qkv_b3, col0):
    B, S, _ = qkv_b3.shape
    a = qkv_b3[:, :, col0:col0 + KV_WIDTH_B]
    a = a.reshape(B, S // CMP_STRIDE, CMP_STRIDE, NSA_KV_GROUPS, HEAD_DIM)
    return a.transpose(0, 3, 1, 2, 4).reshape(B, NSA_KV_GROUPS, S // CMP_STRIDE, CHUNK_FLAT)


def _overlap_t(n_sel, n_cmp_pad):
    cstart = jnp.arange(n_cmp_pad)[None, :] * CMP_STRIDE
    sstart = jnp.arange(n_sel)[:, None] * SEL_BLOCK
    return ((cstart < sstart + SEL_BLOCK) & (cstart + CMP_BLOCK > sstart)).astype(BF16)


def kernel(x, norm1_g, w_in, cmp_pe_k, cmp_w1_k, cmp_w2_k, cmp_pe_v, cmp_w1_v, cmp_w2_v, grp_norm_a,
           grp_norm_b, w_out, norm2_g, w_gate, w_up, w_down, final_g):
    B, S, D = x.shape
    assert D == D_MODEL and S % (DIL_BLOCK * max(d for _, d in DIL_CONFIGS)) == 0
    assert (B * S) % IN_TM == 0 and w_in.shape[0] == 1, "single-layer block only"
    T = B * S
    slopes = jnp.exp2(-8.0 * jnp.arange(1, N_HEADS + 1, dtype=F32) / N_HEADS)
    slopes_a, slopes_b = slopes[0::2], slopes[1::2]
    n_main = COLS_A + COLS_B
    per_group = NSA_HPG * N_GATES

    x2 = x.reshape(T, D)
    w_main = w_in[0][:, :n_main].astype(BF16)
    w_g = w_in[0][:, n_main:].reshape(D, NSA_KV_GROUPS, per_group)
    w_g = jnp.pad(w_g, ((0, 0), (0, 0), (0, LANE - per_group))).reshape(D, GATE_PAD).astype(BF16)
    qkv_a, qkv_b, gates = _in_projection(x2, norm1_g[0][None, :], w_main, w_g)

    dil = [_contiguous_attention(qkv_a, slopes_a, B, S, w) if d == 1
           else _strided_attention(qkv_a, slopes_a, B, S, w, d) for w, d in DIL_CONFIGS]

    b3 = qkv_b.reshape(B, S, COLS_B)
    pe2 = lambda pe: pe.reshape(2, CHUNK_FLAT)
    kc = _compress(_chunk_view(b3, WIDTH_B), pe2(cmp_pe_k[0]), cmp_w1_k[0].astype(BF16),
                   cmp_w2_k[0].astype(BF16))
    vc = _compress(_chunk_view(b3, WIDTH_B + KV_WIDTH_B), pe2(cmp_pe_v[0]), cmp_w1_v[0].astype(BF16),
                   cmp_w2_v[0].astype(BF16))
    col_vs = WIDTH_B + 3 * KV_WIDTH_B
    col_vw = WIDTH_B + 5 * KV_WIDTH_B
    seq_t = lambda c0: b3[:, :, c0:c0 + KV_WIDTH_B].reshape(B, S, NSA_KV_GROUPS, HEAD_DIM).transpose(0, 2, 3, 1)
    o_b = _nsa_attention(qkv_b, kc, vc.transpose(0, 1, 3, 2), seq_t(col_vs), seq_t(col_vw), gates, slopes_b,
                         _overlap_t(S // SEL_BLOCK, kc.shape[2]), B, S)

    h, hn2 = _out_projection([o for o, _ in dil], [s for _, s in dil], o_b, x2,
                             grp_norm_a[0][None, :], grp_norm_b[0][None, :],
                             w_out[0].astype(BF16), norm2_g[0][None, :])
    out = _ffn(hn2, h, w_gate[0].astype(BF16), w_up[0].astype(BF16), w_down[0].astype(BF16),
               final_g[None, :])
    return out.reshape(B, S, D)
```

```python
import functools
import math

import jax
import jax.numpy as jnp
from jax import lax
from jax.experimental import pallas as pl
from jax.experimental.pallas import tpu as pltpu

F32 = jnp.float32
BF16 = jnp.bfloat16

D_MODEL = 2048
HEAD_DIM = 128
N_HEADS = 16
N_HEADS_A = 8
N_HEADS_B = 8
WIDTH_A = N_HEADS_A * HEAD_DIM
WIDTH_B = N_HEADS_B * HEAD_DIM
DIL_CONFIGS = ((128, 1), (512, 4), (2048, 16))
DIL_BLOCK = 128
NSA_KV_GROUPS = 2
NSA_HPG = N_HEADS_B // NSA_KV_GROUPS
KV_WIDTH_B = NSA_KV_GROUPS * HEAD_DIM
CMP_BLOCK = 32
CMP_STRIDE = 16
CMP_HIDDEN = 2 * HEAD_DIM
SEL_BLOCK = 64
SEL_TOP_N = 16
SEL_FORCED_LOCAL = 2
FORCE_BONUS = 1.0e3
WIN_SIZE = 512
N_GATES = 3
FFN_HIDDEN = 5632
RMS_EPS = 1e-6
NEG_INF = -1e30
LOG2_E = math.log2(math.e)

COLS_A = 3 * WIDTH_A
COLS_B = WIDTH_B + 6 * KV_WIDTH_B
LANE = 128
GATE_PAD = NSA_KV_GROUPS * LANE

VMEM_LIMIT = 56 * 1024 * 1024


def _cparams(sem):
    return pltpu.CompilerParams(dimension_semantics=sem, vmem_limit_bytes=VMEM_LIMIT)


def _nt_dot(a, b):
    return lax.dot_general(a, b, (((1,), (1,)), ((), ())), preferred_element_type=F32)


IN_TM = 1024
IN_TN = 512
N_TILES_A = COLS_A // IN_TN
N_TILES_B = COLS_B // IN_TN
Q_B_TILE0 = N_TILES_A


def _inproj_kernel(x_ref, g_ref, w_ref, wg_ref, oa_ref, ob_ref, og_ref, xn_ref, *, scale):
    j = pl.program_id(1)

    @pl.when(j == 0)
    def _():
        x = x_ref[...]
        y = x * lax.rsqrt(jnp.mean(x * x, axis=-1, keepdims=True) + RMS_EPS)
        xn = (y * g_ref[...]).astype(BF16)
        xn_ref[...] = xn
        og_ref[...] = jnp.dot(xn, wg_ref[...], preferred_element_type=F32)

    acc = jnp.dot(xn_ref[...], w_ref[...], preferred_element_type=F32)
    col_scale = jnp.where(j < 2, scale,
                          jnp.where((j >= Q_B_TILE0) & (j < Q_B_TILE0 + 2), scale * LOG2_E, 1.0))
    acc = acc * col_scale.astype(F32)

    @pl.when(j < N_TILES_A)
    def _():
        oa_ref[...] = acc.astype(BF16)

    @pl.when(j >= N_TILES_A)
    def _():
        ob_ref[...] = acc.astype(BF16)


def _in_projection(x2, g, w_main, w_gate):
    T = x2.shape[0]
    grid = (T // IN_TM, N_TILES_A + N_TILES_B)
    return pl.pallas_call(
        functools.partial(_inproj_kernel, scale=1.0 / math.sqrt(HEAD_DIM)),
        grid=grid,
        in_specs=[
            pl.BlockSpec((IN_TM, D_MODEL), lambda i, j: (i, 0)),
            pl.BlockSpec((1, D_MODEL), lambda i, j: (0, 0)),
            pl.BlockSpec((D_MODEL, IN_TN), lambda i, j: (0, j)),
            pl.BlockSpec((D_MODEL, GATE_PAD), lambda i, j: (0, 0)),
        ],
        out_specs=[
            pl.BlockSpec((IN_TM, IN_TN), lambda i, j: (i, jnp.minimum(j, N_TILES_A - 1))),
            pl.BlockSpec((IN_TM, IN_TN), lambda i, j: (i, jnp.maximum(j - N_TILES_A, 0))),
            pl.BlockSpec((IN_TM, GATE_PAD), lambda i, j: (i, 0)),
        ],
        out_shape=[
            jax.ShapeDtypeStruct((T, COLS_A), BF16),
            jax.ShapeDtypeStruct((T, COLS_B), BF16),
            jax.ShapeDtypeStruct((T, GATE_PAD), F32),
        ],
        scratch_shapes=[pltpu.VMEM((IN_TM, D_MODEL), BF16)],
        compiler_params=_cparams(("arbitrary", "arbitrary")),
        name="in_projection",
    )(x2, g, w_main, w_gate)


DIL_HEAD_GROUP = 4
DIL_COLS = DIL_HEAD_GROUP * HEAD_DIM
DIL_LSE_W = (N_HEADS_A // DIL_HEAD_GROUP) * LANE
DIL_PERM = 256


def _dilated_heads(q, k_prev, k_cur, v_prev, v_cur, slopes, has_prev, dilation, span):
    key = lax.broadcasted_iota(jnp.int32, (2 * DIL_BLOCK, DIL_BLOCK), 0)
    qry = lax.broadcasted_iota(jnp.int32, (2 * DIL_BLOCK, DIL_BLOCK), 1)
    rel = qry + DIL_BLOCK - key
    in_prev = jnp.where(rel <= span, jnp.where(has_prev, 0.0, NEG_INF), NEG_INF)
    mask_bias = jnp.where(key < DIL_BLOCK, in_prev, jnp.where(rel >= 0, 0.0, NEG_INF))
    dist = (rel * dilation).astype(F32)
    heads = [slice(h * HEAD_DIM, (h + 1) * HEAD_DIM) for h in range(len(slopes))]
    scores = [_nt_dot(jnp.concatenate([k_prev[:, hs], k_cur[:, hs]], axis=0), q[:, hs]) for hs in heads]
    scores = [s - slope * dist + mask_bias for slope, s in zip(slopes, scores)]
    maxes = [jnp.max(s, axis=0, keepdims=True) for s in scores]
    probs = [jnp.exp(s - m) for s, m in zip(scores, maxes)]
    sums = [jnp.sum(p, axis=0, keepdims=True) for p in probs]
    outs = [lax.dot_general((p * (1.0 / l)).astype(BF16),
                            jnp.concatenate([v_prev[:, hs], v_cur[:, hs]], axis=0),
                            (((0,), (0,)), ((), ())), preferred_element_type=F32)
            for hs, p, l in zip(heads, probs, sums)]
    return (jnp.concatenate(outs, axis=1).astype(BF16),
            [m + jnp.log(l) for m, l in zip(maxes, sums)])


def _lse_tile(rows):
    pad = jnp.zeros((LANE - len(rows), DIL_BLOCK), F32)
    return jnp.concatenate(list(rows) + [pad], axis=0).T


def _dilated_kernel(slopes_ref, q_ref, kc_ref, kp_ref, vc_ref, vp_ref, o_ref, lse_ref, *,
                    dilation, span):
    slopes = [slopes_ref[h] for h in range(N_HEADS_A)]
    o, lse = _dilated_heads(q_ref[0], kp_ref[0], kc_ref[0], vp_ref[0], vc_ref[0], slopes,
                            pl.program_id(2) > 0, dilation, span)
    o_ref[0] = o
    for grp in range(N_HEADS_A // DIL_HEAD_GROUP):
        lse_ref[0, :, grp * LANE:(grp + 1) * LANE] = _lse_tile(
            lse[grp * DIL_HEAD_GROUP:(grp + 1) * DIL_HEAD_GROUP])


def _split_bf16(x, pieces):
    out, rest = [], x
    for _ in range(pieces):
        part = rest.astype(BF16)
        out.append(part)
        rest = rest - part.astype(F32)
    return out


def _strided_kernel(slopes_ref, perm_ref, perm_t_ref, q_ref, k_ref, v_ref, o_ref, lse_ref,
                    qp_ref, kp_ref, vp_ref, op_ref, lp_ref, *, dilation, span):
    grp = pl.program_id(1)
    n = pl.program_id(2)
    n_sub = q_ref.shape[1] // DIL_PERM
    width = DIL_PERM // dilation
    cur = n & 1
    prev = 1 - cur
    perm = perm_ref[...]

    def deinterleave(x):
        y = jnp.dot(perm, x, preferred_element_type=F32).astype(BF16)
        return y.reshape(dilation, width, x.shape[1])

    for sub in range(n_sub):
        rows = slice(sub * DIL_PERM, (sub + 1) * DIL_PERM)
        qp_ref[:, sub] = deinterleave(q_ref[0, rows, :])
        kp_ref[cur, :, sub] = deinterleave(k_ref[0, rows, :])
        vp_ref[cur, :, sub] = deinterleave(v_ref[0, rows, :])

    @pl.when(n == 0)
    def _():
        kp_ref[prev] = jnp.zeros(kp_ref.shape[1:], BF16)
        vp_ref[prev] = jnp.zeros(vp_ref.shape[1:], BF16)

    slopes = [slopes_ref[grp * DIL_HEAD_GROUP + i] for i in range(DIL_HEAD_GROUP)]
    blk_rows = lambda a: a.reshape(DIL_BLOCK, a.shape[-1])

    def residues(i, carry):
        for r in (2 * i, 2 * i + 1):
            o, lse = _dilated_heads(blk_rows(qp_ref[r]), blk_rows(kp_ref[prev, r]), blk_rows(kp_ref[cur, r]),
                                    blk_rows(vp_ref[prev, r]), blk_rows(vp_ref[cur, r]), slopes, n > 0,
                                    dilation, span)
            op_ref[r] = o.reshape(n_sub, width, DIL_COLS)
            lp_ref[r] = _lse_tile(lse).reshape(n_sub, width, LANE)
        return carry

    lax.fori_loop(0, dilation // 2, residues, 0)

    perm_t = perm_t_ref[...]
    for sub in range(n_sub):
        rows = slice(sub * DIL_PERM, (sub + 1) * DIL_PERM)
        o_sub = op_ref[:, sub].reshape(DIL_PERM, DIL_COLS)
        o_ref[0, rows, :] = jnp.dot(perm_t, o_sub, preferred_element_type=F32).astype(BF16)
        l_sub = lp_ref[:, sub].reshape(DIL_PERM, LANE)
        lse_ref[0, rows, :] = sum(jnp.dot(perm_t, part, preferred_element_type=F32)
                                  for part in _split_bf16(l_sub, 3))


def _contiguous_attention(qkv_a, slopes_a, B, S, window):
    nb = S // DIL_BLOCK
    a3 = qkv_a.reshape(B, S, COLS_A)
    blk = (1, DIL_BLOCK, WIDTH_A)
    prev = lambda n: jnp.maximum(n - 1, 0)
    o, lse = pl.pallas_call(
        functools.partial(_dilated_kernel, dilation=1, span=window),
        grid=(B, 1, nb),
        in_specs=[
            pl.BlockSpec(memory_space=pltpu.SMEM),
            pl.BlockSpec(blk, lambda b, r, n: (b, n, 0)),
            pl.BlockSpec(blk, lambda b, r, n: (b, n, 1)),
            pl.BlockSpec(blk, lambda b, r, n: (b, prev(n), 1)),
            pl.BlockSpec(blk, lambda b, r, n: (b, n, 2)),
            pl.BlockSpec(blk, lambda b, r, n: (b, prev(n), 2)),
        ],
        out_specs=[
            pl.BlockSpec(blk, lambda b, r, n: (b, n, 0)),
            pl.BlockSpec((1, DIL_BLOCK, DIL_LSE_W), lambda b, r, n: (b, n, 0)),
        ],
        out_shape=[
            jax.ShapeDtypeStruct((B, S, WIDTH_A), BF16),
            jax.ShapeDtypeStruct((B, S, DIL_LSE_W), F32),
        ],
        compiler_params=_cparams(("arbitrary", "arbitrary", "arbitrary")),
        name="dilated_attention_d1",
    )(slopes_a, a3, a3, a3, a3, a3)
    return o.reshape(B * S, WIDTH_A), lse.reshape(B * S, DIL_LSE_W)


def _permutation(dilation):
    width = DIL_PERM // dilation
    out_row = jnp.arange(DIL_PERM)
    src = (out_row % width) * dilation + out_row // width
    return (src[:, None] == jnp.arange(DIL_PERM)[None, :]).astype(BF16)


def _strided_attention(qkv_a, slopes_a, B, S, window, dilation):
    rows = DIL_BLOCK * dilation
    assert S % rows == 0 and rows % DIL_PERM == 0 and DIL_PERM % dilation == 0
    assert (DIL_PERM // dilation) % 16 == 0
    assert dilation % 2 == 0
    n_sub, width = rows // DIL_PERM, DIL_PERM // dilation
    n_grp = N_HEADS_A // DIL_HEAD_GROUP
    per = WIDTH_A // DIL_COLS
    a3 = qkv_a.reshape(B, S, COLS_A)
    perm = _permutation(dilation)
    blk = (1, rows, DIL_COLS)
    whole = lambda b, g, n: (0, 0)
    packed = lambda dt: pltpu.VMEM((dilation, n_sub, width, DIL_COLS), dt)
    o, lse = pl.pallas_call(
        functools.partial(_strided_kernel, dilation=dilation, span=window // dilation),
        grid=(B, n_grp, S // rows),
        in_specs=[
            pl.BlockSpec(memory_space=pltpu.SMEM),
            pl.BlockSpec((DIL_PERM, DIL_PERM), whole),
            pl.BlockSpec((DIL_PERM, DIL_PERM), whole),
            pl.BlockSpec(blk, lambda b, g, n: (b, n, g)),
            pl.BlockSpec(blk, lambda b, g, n: (b, n, per + g)),
            pl.BlockSpec(blk, lambda b, g, n: (b, n, 2 * per + g)),
        ],
        out_specs=[
            pl.BlockSpec(blk, lambda b, g, n: (b, n, g)),
            pl.BlockSpec((1, rows, LANE), lambda b, g, n: (b, n, g)),
        ],
        out_shape=[
            jax.ShapeDtypeStruct((B, S, WIDTH_A), BF16),
            jax.ShapeDtypeStruct((B, S, DIL_LSE_W), F32),
        ],
        scratch_shapes=[
            packed(BF16),
            pltpu.VMEM((2, dilation, n_sub, width, DIL_COLS), BF16),
            pltpu.VMEM((2, dilation, n_sub, width, DIL_COLS), BF16),
            packed(BF16),
            pltpu.VMEM((dilation, n_sub, width, LANE), F32),
        ],
        compiler_params=_cparams(("arbitrary", "arbitrary", "arbitrary")),
        name=f"dilated_attention_d{dilation}",
    )(slopes_a, perm, perm.T, a3, a3, a3)
    return o.reshape(B * S, WIDTH_A), lse.reshape(B * S, DIL_LSE_W)


CHUNK_FLAT = CMP_STRIDE * HEAD_DIM


def _compress_kernel(x_ref, pe_ref, w1_ref, w2_ref, o_ref):
    x = x_ref[0, 0].astype(F32)
    first = jnp.dot((x + pe_ref[0:1, :]).astype(BF16), w1_ref[0:CHUNK_FLAT, :],
                    preferred_element_type=F32)
    second = jnp.dot((x + pe_ref[1:2, :]).astype(BF16), w1_ref[CHUNK_FLAT:2 * CHUNK_FLAT, :],
                     preferred_element_type=F32)
    nc = x.shape[0]
    hidden = first + pltpu.roll(second, shift=nc - 1, axis=0)
    act = hidden * jax.nn.sigmoid(hidden)
    o_ref[0, 0] = jnp.dot(act.astype(BF16), w2_ref[...], preferred_element_type=F32).astype(BF16)


def _compress(chunks, pe2, w1, w2):
    B, G, nc, _ = chunks.shape
    return pl.pallas_call(
        _compress_kernel,
        grid=(B, G),
        in_specs=[
            pl.BlockSpec((1, 1, nc, CHUNK_FLAT), lambda b, g: (b, g, 0, 0)),
            pl.BlockSpec((2, CHUNK_FLAT), lambda b, g: (0, 0)),
            pl.BlockSpec((2 * CHUNK_FLAT, CMP_HIDDEN), lambda b, g: (0, 0)),
            pl.BlockSpec((CMP_HIDDEN, HEAD_DIM), lambda b, g: (0, 0)),
        ],
        out_specs=pl.BlockSpec((1, 1, nc, HEAD_DIM), lambda b, g: (b, g, 0, 0)),
        out_shape=jax.ShapeDtypeStruct((B, G, nc, HEAD_DIM), BF16),
        compiler_params=_cparams(("arbitrary", "arbitrary")),
        name="nsa_compress",
    )(chunks, pe2, w1, w2)


NSA_TQ = 256
NSA_TK = 512
WIN_SPAN = WIN_SIZE + NSA_TQ
SLOPE_PIECES = 3


def _stack_heads(a):
    return jnp.concatenate([a[:, h * HEAD_DIM:(h + 1) * HEAD_DIM] for h in range(NSA_HPG)], axis=0)


def _nsa_kernel(slopes_ref, q_ref, kc_ref, vct_ref, ks_ref, vst_ref, kw_ref, vwt_ref, kf_ref, gate_ref,
                ovl_ref, o_ref, *, n_sel):
    g = pl.program_id(1)
    qi = pl.program_id(2)
    t0 = qi * NSA_TQ
    cols = NSA_HPG * NSA_TQ

    q = _stack_heads(q_ref[0])
    slope_row = jnp.concatenate(
        [jnp.full((1, NSA_TQ), slopes_ref[SLOPE_PIECES, g * NSA_HPG + h], F32) for h in range(NSA_HPG)],
        axis=1)
    t_row = t0 + (lax.broadcasted_iota(jnp.int32, (1, cols), 1) & (NSA_TQ - 1))

    n_cmp_pad = kc_ref.shape[2]
    cend = (lax.broadcasted_iota(jnp.int32, (n_cmp_pad, 1), 0) * CMP_STRIDE + (CMP_BLOCK - 1))
    rel_c = t_row - cend
    ok_c = rel_c >= 0
    s = _nt_dot(kc_ref[0, 0], q)
    s = s - slope_row * rel_c.astype(F32)
    s = jnp.where(ok_c, s, NEG_INF)
    m = jnp.max(s, axis=0, keepdims=True)
    e = jnp.where(ok_c, jnp.exp2(s - m), 0.0)
    p = e / jnp.maximum(jnp.sum(e, axis=0, keepdims=True), 1e-30)
    o_cmp = jnp.dot(vct_ref[0, 0], p.astype(BF16), preferred_element_type=F32)

    lane = lax.broadcasted_iota(jnp.int32, (1, LANE), 1)
    slope_feat = []
    for h in range(NSA_HPG):
        row = jnp.zeros((1, LANE), F32)
        for part in range(2):
            for piece in range(SLOPE_PIECES):
                row = jnp.where(lane == n_sel + part * SLOPE_PIECES + piece,
                                slopes_ref[piece, g * NSA_HPG + h], row)
        slope_feat.append(jnp.broadcast_to(row, (NSA_TQ, LANE)))
    slope_feat = jnp.concatenate(slope_feat, axis=0)
    q_win = jnp.concatenate([q, slope_feat.astype(BF16)], axis=1)

    def keys_aug(k_ref, start, size):
        return jnp.concatenate([k_ref[0, pl.ds(start, size), :], kf_ref[pl.ds(start, size), :]], axis=1)

    n_pieces = WIN_SIZE // NSA_TQ + 1
    win_scores, win_starts = [], []
    for i in range(n_pieces):
        first = t0 - WIN_SIZE + i * NSA_TQ
        start = pl.multiple_of(jnp.maximum(first, 0), NSA_TQ)
        s_i = _nt_dot(keys_aug(kw_ref, start, NSA_TQ), q_win)
        rel_i = t_row - (first + lax.broadcasted_iota(jnp.int32, (NSA_TQ, 1), 0))
        if i == n_pieces - 1:
            s_i = jnp.where(rel_i >= 0, s_i, NEG_INF)
        elif i == 0:
            s_i = jnp.where(rel_i < jnp.where(first >= 0, WIN_SIZE, -WIN_SPAN), s_i, NEG_INF)
        else:
            s_i = s_i + jnp.where(first >= 0, 0.0, NEG_INF)
        win_scores.append(s_i)
        win_starts.append(start)
    m_w = functools.reduce(jnp.maximum, [jnp.max(s_i, axis=0, keepdims=True) for s_i in win_scores])
    win_probs = [jnp.exp2(s_i - m_w) for s_i in win_scores]
    l_w = sum(jnp.sum(p_i, axis=0, keepdims=True) for p_i in win_probs)
    o_win = sum(jnp.dot(vwt_ref[0, 0, :, pl.ds(start, NSA_TQ)], p_i.astype(BF16), preferred_element_type=F32)
                for start, p_i in zip(win_starts, win_probs)) / l_w

    p_grp = p[:, 0:NSA_TQ]
    for h in range(1, NSA_HPG):
        p_grp = p_grp + p[:, h * NSA_TQ:(h + 1) * NSA_TQ]
    p_hi = p_grp.astype(BF16)
    p_lo = (p_grp - p_hi.astype(F32)).astype(BF16)
    ovl = ovl_ref[...]
    imp_t = (jnp.dot(ovl, p_hi, preferred_element_type=F32)
             + jnp.dot(ovl, p_lo, preferred_element_type=F32))
    blk = lax.broadcasted_iota(jnp.int32, (n_sel, NSA_TQ), 0)
    cur = (t0 + lax.broadcasted_iota(jnp.int32, (n_sel, NSA_TQ), 1)) // SEL_BLOCK
    valid = blk <= cur
    forced = (blk == 0) | (valid & (blk > cur - SEL_FORCED_LOCAL))
    score = jnp.where(valid, imp_t + jnp.where(forced, FORCE_BONUS, 0.0), -1.0)
    rank = jnp.zeros((n_sel, NSA_TQ), jnp.int32)
    for mth in range(n_sel):
        other = score[mth:mth + 1, :]
        tie_first = jnp.where(blk > mth, 1, 0)
        rank = rank + jnp.where(other > score, 1, jnp.where(other == score, tie_first, 0))
    member_t = jnp.where(valid, jnp.where(rank < SEL_TOP_N, 1.0, 0.0), 0.0).astype(BF16)
    qrow =lax.broadcasted_iota(jnp.int32, (cols, NSA_TQ), 0) & (NSA_TQ - 1)
    eye = jnp.where(qrow == lax.broadcasted_iota(jnp.int32, (cols, NSA_TQ), 1), 1.0, 0.0).astype(BF16)
    member_pad = jnp.concatenate([member_t, jnp.ones((LANE - n_sel, NSA_TQ), BF16)], axis=0)
    picked = _nt_dot(eye, member_pad)
    q_sel = jnp.concatenate([q, (jnp.where(picked > 0.5, 0.0, NEG_INF) + slope_feat).astype(BF16)], axis=1)

    def sel_step(j, carry, causal):
        m_i, l_i, acc = carry
        start = pl.multiple_of(j * NSA_TK, NSA_TK)
        s_ = _nt_dot(keys_aug(ks_ref, start, NSA_TK), q_sel)
        if causal:
            pos = start + lax.broadcasted_iota(jnp.int32, (NSA_TK, 1), 0)
            s_ = jnp.where(pos <= t_row, s_, NEG_INF)
        m_new = jnp.maximum(m_i, jnp.max(s_, axis=0, keepdims=True))
        alpha = jnp.exp2(m_i - m_new)
        p_ = jnp.exp2(s_ - m_new)
        l_new = alpha * l_i + jnp.sum(p_, axis=0, keepdims=True)
        acc_new = alpha * acc + jnp.dot(vst_ref[0, 0, :, pl.ds(start, NSA_TK)], p_.astype(BF16),
                                        preferred_element_type=F32)
        return m_new, l_new, acc_new

    init = (jnp.full((1, cols), NEG_INF, F32), jnp.zeros((1, cols), F32),
            jnp.zeros((HEAD_DIM, cols), F32))
    last = (t0 + NSA_TQ - 1) // NSA_TK
    carry = lax.fori_loop(0, last, functools.partial(sel_step, causal=False), init)
    _, l_s, acc_s = sel_step(last, carry, causal=True)
    o_sel = acc_s / l_s

    gates_t = jax.nn.sigmoid(gate_ref[0]).T
    for h in range(NSA_HPG):
        cs = slice(h * NSA_TQ, (h + 1) * NSA_TQ)
        gsel = [gates_t[h * N_GATES + k:h * N_GATES + k + 1, :] for k in range(N_GATES)]
        mix_t = gsel[0] * o_cmp[:, cs] + gsel[1] * o_sel[:, cs] + gsel[2] * o_win[:, cs]
        o_ref[0, :, h * HEAD_DIM:(h + 1) * HEAD_DIM] = mix_t.T


def _key_features(S):
    pos = jnp.arange(S)[:, None]
    lane = jnp.arange(LANE)[None, :]
    n_sel = S // SEL_BLOCK
    feat = jnp.where(lane == pos // SEL_BLOCK, 1, 0)
    feat = jnp.where((lane >= n_sel) & (lane < n_sel + SLOPE_PIECES), pos % SEL_BLOCK, feat)
    feat = jnp.where((lane >= n_sel + SLOPE_PIECES) & (lane < n_sel + 2 * SLOPE_PIECES),
                     (pos // SEL_BLOCK) * SEL_BLOCK, feat)
    return feat.astype(BF16)


def _slope_table(slopes):
    rows, rest = [], slopes
    for _ in range(SLOPE_PIECES):
        piece = rest.astype(BF16).astype(F32)
        rows.append(piece)
        rest = rest - piece
    return jnp.stack(rows + [slopes])


def _nsa_attention(qkv_b, kc, vc_t, vs_t, vw_t, gates, slopes_b, overlap_t, B, S):
    assert S % NSA_TK == 0 and S >= WIN_SPAN
    assert S // SEL_BLOCK <= SEL_BLOCK and S // SEL_BLOCK + 2 * SLOPE_PIECES <= LANE
    b3 = qkv_b.reshape(B, S, COLS_B)
    g3 = gates.reshape(B, S, GATE_PAD)
    n_cmp_pad = kc.shape[2]
    n_sel = S // SEL_BLOCK
    q_w = NSA_HPG * HEAD_DIM
    kv0 = WIDTH_B // HEAD_DIM
    per = KV_WIDTH_B // HEAD_DIM
    seq_blk = (1, S, HEAD_DIM)
    seq_t_blk = (1, 1, HEAD_DIM, S)
    grp = lambda b, g, i: (b, g, 0, 0)
    out = pl.pallas_call(
        functools.partial(_nsa_kernel, n_sel=n_sel),
        grid=(B, NSA_KV_GROUPS, S // NSA_TQ),
        in_specs=[
            pl.BlockSpec(memory_space=pltpu.SMEM),
            pl.BlockSpec((1, NSA_TQ, q_w), lambda b, g, i: (b, i, g)),
            pl.BlockSpec((1, 1, n_cmp_pad, HEAD_DIM), grp),
            pl.BlockSpec((1, 1, HEAD_DIM, n_cmp_pad), grp),
            pl.BlockSpec(seq_blk, lambda b, g, i: (b, 0, kv0 + 2 * per + g)),
            pl.BlockSpec(seq_t_blk, grp),
            pl.BlockSpec(seq_blk, lambda b, g, i: (b, 0, kv0 + 4 * per + g)),
            pl.BlockSpec(seq_t_blk, grp),
            pl.BlockSpec((S, LANE), lambda b, g, i: (0, 0)),
            pl.BlockSpec((1, NSA_TQ, LANE), lambda b, g, i: (b, i, g)),
            pl.BlockSpec((n_sel, n_cmp_pad), lambda b, g, i: (0, 0)),
        ],
        out_specs=pl.BlockSpec((1, NSA_TQ, q_w), lambda b, g, i: (b, i, g)),
        out_shape=jax.ShapeDtypeStruct((B, S, WIDTH_B), F32),
        compiler_params=_cparams(("arbitrary", "arbitrary", "arbitrary")),
        name="nsa_attention",
    )(_slope_table(slopes_b * LOG2_E), b3, kc, vc_t, b3, vs_t, b3, vw_t, _key_features(S), g3, overlap_t)
    return out.reshape(B * S, WIDTH_B)


OUT_TM = 256


def _rms(x, g):
    return x * lax.rsqrt(jnp.mean(x * x, axis=-1, keepdims=True) + RMS_EPS) * g


def _outproj_kernel(o1_ref, o2_ref, o3_ref, l1_ref, l2_ref, l3_ref, ob_ref, x_ref, ga_ref, gb_ref,
                    w_ref, g2_ref, h_ref, hn_ref):
    l1, l2, l3 = l1_ref[...], l2_ref[...], l3_ref[...]
    mx = jnp.maximum(jnp.maximum(l1, l2), l3)
    e1, e2, e3 = jnp.exp(l1 - mx), jnp.exp(l2 - mx), jnp.exp(l3 - mx)
    den = e1 + e2 + e3
    w1, w2, w3 = e1 / den, e2 / den, e3 / den
    parts = []
    for h in range(N_HEADS_A):
        hs = slice(h * HEAD_DIM, (h + 1) * HEAD_DIM)
        c = (h // DIL_HEAD_GROUP) * LANE + h % DIL_HEAD_GROUP
        parts.append(w1[:, c:c + 1] * o1_ref[:, hs] + w2[:, c:c + 1] * o2_ref[:, hs]
                     + w3[:, c:c + 1] * o3_ref[:, hs])
    o_a = jnp.concatenate(parts, axis=-1)
    mixed = jnp.concatenate([_rms(o_a, ga_ref[...]), _rms(ob_ref[...], gb_ref[...])], axis=-1)
    h = x_ref[...] + jnp.dot(mixed.astype(BF16), w_ref[...], preferred_element_type=F32)
    h_ref[...] = h
    hn_ref[...] = _rms(h, g2_ref[...]).astype(BF16)


def _out_projection(o_as, lses, o_b, x2, ga, gb, w_out, g2):
    T = x2.shape[0]
    row = lambda w: pl.BlockSpec((OUT_TM, w), lambda i: (i, 0))
    full = lambda r, c: pl.BlockSpec((r, c), lambda i: (0, 0))
    return pl.pallas_call(
        _outproj_kernel,
        grid=(T // OUT_TM,),
        in_specs=[row(WIDTH_A)] * 3 + [row(DIL_LSE_W)] * 3 + [row(WIDTH_B), row(D_MODEL),
                  full(1, WIDTH_A), full(1, WIDTH_B), full(WIDTH_A + WIDTH_B, D_MODEL), full(1, D_MODEL)],
        out_specs=[row(D_MODEL), row(D_MODEL)],
        out_shape=[jax.ShapeDtypeStruct((T, D_MODEL), F32), jax.ShapeDtypeStruct((T, D_MODEL), BF16)],
        compiler_params=_cparams(("arbitrary",)),
        name="out_projection",
    )(*o_as, *lses, o_b, x2, ga, gb, w_out, g2)


FFN_TM = 512
FFN_TH = 512


def _ffn_kernel(hn_ref, h_ref, wg_ref, wu_ref, wd_ref, gf_ref, o_ref, acc_ref):
    j = pl.program_id(1)
    hn = hn_ref[...]
    gate = jnp.dot(hn, wg_ref[...], preferred_element_type=F32)
    up = jnp.dot(hn, wu_ref[...], preferred_element_type=F32)
    act = (gate * jax.nn.sigmoid(gate) * up).astype(BF16)
    part = jnp.dot(act, wd_ref[...], preferred_element_type=F32)

    @pl.when(j == 0)
    def _():
        acc_ref[...] = part

    @pl.when(j > 0)
    def _():
        acc_ref[...] += part

    @pl.when(j == pl.num_programs(1) - 1)
    def _():
        o_ref[...] = _rms(h_ref[...] + acc_ref[...], gf_ref[...])


def _ffn(hn2, h, w_gate, w_up, w_down, gf):
    T = h.shape[0]
    return pl.pallas_call(
        _ffn_kernel,
        grid=(T // FFN_TM, FFN_HIDDEN // FFN_TH),
        in_specs=[
            pl.BlockSpec((FFN_TM, D_MODEL), lambda i, j: (i, 0)),
            pl.BlockSpec((FFN_TM, D_MODEL), lambda i, j: (i, 0)),
            pl.BlockSpec((D_MODEL, FFN_TH), lambda i, j: (0, j)),
            pl.BlockSpec((D_MODEL, FFN_TH), lambda i, j: (0, j)),
            pl.BlockSpec((FFN_TH, D_MODEL), lambda i, j: (j, 0)),
            pl.BlockSpec((1, D_MODEL), lambda i, j: (0, 0)),
        ],
        out_specs=pl.BlockSpec((FFN_TM, D_MODEL), lambda i, j: (i, 0)),
        out_shape=jax.ShapeDtypeStruct((T, D_MODEL), F32),
        scratch_shapes=[pltpu.VMEM((FFN_TM, D_MODEL), F32)],
        compiler_params=_cparams(("arbitrary", "arbitrary")),
        name="swiglu_ffn",
    )(hn2, h, w_gate, w_up, w_down, gf)


def _chunk_view(qkv_b3, col0):
    B, S, _ = qkv_b3.shape
    a = qkv_b3[:, :, col0:col0 + KV_WIDTH_B]
    a = a.reshape(B, S // CMP_STRIDE, CMP_STRIDE, NSA_KV_GROUPS, HEAD_DIM)
    return a.transpose(0, 3, 1, 2, 4).reshape(B, NSA_KV_GROUPS, S // CMP_STRIDE, CHUNK_FLAT)


def _overlap_t(n_sel, n_cmp_pad):
    cstart = jnp.arange(n_cmp_pad)[None, :] * CMP_STRIDE
    sstart = jnp.arange(n_sel)[:, None] * SEL_BLOCK
    return ((cstart < sstart + SEL_BLOCK) & (cstart + CMP_BLOCK > sstart)).astype(BF16)


def kernel(x, norm1_g, w_in, cmp_pe_k, cmp_w1_k, cmp_w2_k, cmp_pe_v, cmp_w1_v, cmp_w2_v, grp_norm_a,
           grp_norm_b, w_out, norm2_g, w_gate, w_up, w_down, final_g):
    B, S, D = x.shape
    assert D == D_MODEL and S % (DIL_BLOCK * max(d for _, d in DIL_CONFIGS)) == 0
    assert (B * S) % IN_TM == 0 and w_in.shape[0] == 1, "single-layer block only"
    T = B * S
    slopes = jnp.exp2(-8.0 * jnp.arange(1, N_HEADS + 1, dtype=F32) / N_HEADS)
    slopes_a, slopes_b = slopes[0::2], slopes[1::2]
    n_main = COLS_A + COLS_B
    per_group = NSA_HPG * N_GATES

    x2 = x.reshape(T, D)
    w_main = w_in[0][:, :n_main].astype(BF16)
    w_g = w_in[0][:, n_main:].reshape(D, NSA_KV_GROUPS, per_group)
    w_g = jnp.pad(w_g, ((0, 0), (0, 0), (0, LANE - per_group))).reshape(D, GATE_PAD).astype(BF16)
    qkv_a, qkv_b, gates = _in_projection(x2, norm1_g[0][None, :], w_main, w_g)

    dil = [_contiguous_attention(qkv_a, slopes_a, B, S, w) if d == 1
           else _strided_attention(qkv_a, slopes_a, B, S, w, d) for w, d in DIL_CONFIGS]

    b3 = qkv_b.reshape(B, S, COLS_B)
    pe2 = lambda pe: pe.reshape(2, CHUNK_FLAT)
    kc = _compress(_chunk_view(b3, WIDTH_B), pe2(cmp_pe_k[0]), cmp_w1_k[0].astype(BF16),
                   cmp_w2_k[0].astype(BF16))
    vc = _compress(_chunk_view(b3, WIDTH_B + KV_WIDTH_B), pe2(cmp_pe_v[0]), cmp_w1_v[0].astype(BF16),
                   cmp_w2_v[0].astype(BF16))
    col_vs = WIDTH_B + 3 * KV_WIDTH_B
    col_vw = WIDTH_B + 5 * KV_WIDTH_B
    seq_t = lambda c0: b3[:, :, c0:c0 + KV_WIDTH_B].reshape(B, S, NSA_KV_GROUPS, HEAD_DIM).transpose(0, 2, 3, 1)
    o_b = _nsa_attention(qkv_b, kc, vc.transpose(0, 1, 3, 2), seq_t(col_vs), seq_t(col_vw), gates, slopes_b,
                         _overlap_t(S // SEL_BLOCK, kc.shape[2]), B, S)

    h, hn2 = _out_projection([o for o, _ in dil], [s for _, s in dil], o_b, x2,
                             grp_norm_a[0][None, :], grp_norm_b[0][None, :],
                             w_out[0].astype(BF16), norm2_g[0][None, :])
    out = _ffn(hn2, h, w_gate[0].astype(BF16), w_up[0].astype(BF16), w_down[0].astype(BF16),
               final_g[None, :])
    return out.reshape(B, S, D)
```

```python
import functools
import math

import jax
import jax.numpy as jnp
from jax import lax
from jax.experimental import pallas as pl
from jax.experimental.pallas import tpu as pltpu

F32 = jnp.float32
BF16 = jnp.bfloat16

D_MODEL = 2048
HEAD_DIM = 128
N_HEADS = 16
N_HEADS_A = 8
N_HEADS_B = 8
WIDTH_A = N_HEADS_A * HEAD_DIM
WIDTH_B = N_HEADS_B * HEAD_DIM
DIL_CONFIGS = ((128, 1), (512, 4), (2048, 16))
DIL_BLOCK = 128
NSA_KV_GROUPS = 2
NSA_HPG = N_HEADS_B // NSA_KV_GROUPS
KV_WIDTH_B = NSA_KV_GROUPS * HEAD_DIM
CMP_BLOCK = 32
CMP_STRIDE = 16
CMP_HIDDEN = 2 * HEAD_DIM
SEL_BLOCK = 64
SEL_TOP_N = 16
SEL_FORCED_LOCAL = 2
FORCE_BONUS = 1.0e3
WIN_SIZE = 512
N_GATES = 3
FFN_HIDDEN = 5632
RMS_EPS = 1e-6
NEG_INF = -1e30
LOG2_E = math.log2(math.e)

COLS_A = 3 * WIDTH_A
COLS_B = WIDTH_B + 6 * KV_WIDTH_B
COLS_MAIN = COLS_A + COLS_B
LANE = 128
GATE_PAD = NSA_KV_GROUPS * LANE

VMEM_LIMIT = 56 * 1024 * 1024


def _cparams(sem):
    return pltpu.CompilerParams(dimension_semantics=sem, vmem_limit_bytes=VMEM_LIMIT)


def _nt_dot(a, b):
    return lax.dot_general(a, b, (((1,), (1,)), ((), ())), preferred_element_type=F32)


IN_TM = 1024
IN_TN = 512
N_TILES_A = COLS_A // IN_TN
N_TILES_B = COLS_B // IN_TN
Q_B_TILE0 = N_TILES_A


def _inproj_kernel(x_ref, g_ref, w_ref, wg_ref, o_ref, og_ref, xn_ref, *, scale):
    j = pl.program_id(1)

    @pl.when(j == 0)
    def _():
        x = x_ref[...]
        y = x * lax.rsqrt(jnp.mean(x * x, axis=-1, keepdims=True) + RMS_EPS)
        xn = (y * g_ref[...]).astype(BF16)
        xn_ref[...] = xn
        og_ref[...] = jnp.dot(xn, wg_ref[...], preferred_element_type=F32)

    acc = jnp.dot(xn_ref[...], w_ref[...], preferred_element_type=F32)
    col_scale = jnp.where(j < 2, scale,
                          jnp.where((j >= Q_B_TILE0) & (j < Q_B_TILE0 + 2), scale * LOG2_E, 1.0))
    o_ref[...] = (acc * col_scale.astype(F32)).astype(BF16)


def _in_projection(x2, g, w_main, w_gate):
    T = x2.shape[0]
    grid = (T // IN_TM, N_TILES_A + N_TILES_B)
    return pl.pallas_call(
        functools.partial(_inproj_kernel, scale=1.0 / math.sqrt(HEAD_DIM)),
        grid=grid,
        in_specs=[
            pl.BlockSpec((IN_TM, D_MODEL), lambda i, j: (i, 0)),
            pl.BlockSpec((1, D_MODEL), lambda i, j: (0, 0)),
            pl.BlockSpec((D_MODEL, IN_TN), lambda i, j: (0, j)),
            pl.BlockSpec((D_MODEL, GATE_PAD), lambda i, j: (0, 0)),
        ],
        out_specs=[
            pl.BlockSpec((IN_TM, IN_TN), lambda i, j: (i, j)),
            pl.BlockSpec((IN_TM, GATE_PAD), lambda i, j: (i, 0)),
        ],
        out_shape=[
            jax.ShapeDtypeStruct((T, COLS_MAIN), BF16),
            jax.ShapeDtypeStruct((T, GATE_PAD), F32),
        ],
        scratch_shapes=[pltpu.VMEM((IN_TM, D_MODEL), BF16)],
        compiler_params=_cparams(("arbitrary", "arbitrary")),
        name="in_projection",
    )(x2, g, w_main, w_gate)


DIL_HEAD_GROUP = 4
DIL_COLS = DIL_HEAD_GROUP * HEAD_DIM
DIL_LSE_W = (N_HEADS_A // DIL_HEAD_GROUP) * LANE
DIL_PERM = 256


def _dilated_heads(q, k_prev, k_cur, v_prev, v_cur, slopes, has_prev, dilation, span):
    key = lax.broadcasted_iota(jnp.int32, (2 * DIL_BLOCK, DIL_BLOCK), 0)
    qry = lax.broadcasted_iota(jnp.int32, (2 * DIL_BLOCK, DIL_BLOCK), 1)
    rel = qry + DIL_BLOCK - key
    in_prev = jnp.where(rel <= span, jnp.where(has_prev, 0.0, NEG_INF), NEG_INF)
    mask_bias = jnp.where(key < DIL_BLOCK, in_prev, jnp.where(rel >= 0, 0.0, NEG_INF))
    dist = (rel * dilation).astype(F32)
    heads = [slice(h * HEAD_DIM, (h + 1) * HEAD_DIM) for h in range(len(slopes))]
    scores = [_nt_dot(jnp.concatenate([k_prev[:, hs], k_cur[:, hs]], axis=0), q[:, hs]) for hs in heads]
    scores = [s - slope * dist + mask_bias for slope, s in zip(slopes, scores)]
    maxes = [jnp.max(s, axis=0, keepdims=True) for s in scores]
    probs = [jnp.exp(s - m) for s, m in zip(scores, maxes)]
    sums = [jnp.sum(p, axis=0, keepdims=True) for p in probs]
    outs = [lax.dot_general((p * (1.0 / l)).astype(BF16),
                            jnp.concatenate([v_prev[:, hs], v_cur[:, hs]], axis=0),
                            (((0,), (0,)), ((), ())), preferred_element_type=F32)
            for hs, p, l in zip(heads, probs, sums)]
    return (jnp.concatenate(outs, axis=1).astype(BF16),
            [m + jnp.log(l) for m, l in zip(maxes, sums)])


def _lse_tile(rows):
    pad = jnp.zeros((LANE - len(rows), DIL_BLOCK), F32)
    return jnp.concatenate(list(rows) + [pad], axis=0).T


def _dilated_kernel(slopes_ref, q_ref, kc_ref, kp_ref, vc_ref, vp_ref, o_ref, lse_ref, *,
                    dilation, span):
    slopes = [slopes_ref[h] for h in range(N_HEADS_A)]
    o, lse = _dilated_heads(q_ref[0], kp_ref[0], kc_ref[0], vp_ref[0], vc_ref[0], slopes,
                            pl.program_id(2) > 0, dilation, span)
    o_ref[0] = o
    for grp in range(N_HEADS_A // DIL_HEAD_GROUP):
        lse_ref[0, :, grp * LANE:(grp + 1) * LANE] = _lse_tile(
            lse[grp * DIL_HEAD_GROUP:(grp + 1) * DIL_HEAD_GROUP])


def _split_bf16(x, pieces):
    out, rest = [], x
    for _ in range(pieces):
        part = rest.astype(BF16)
        out.append(part)
        rest = rest - part.astype(F32)
    return out


def _strided_kernel(slopes_ref, perm_ref, perm_t_ref, q_ref, k_ref, v_ref, o_ref, lse_ref,
                    qp_ref, kp_ref, vp_ref, op_ref, lp_ref, *, dilation, span):
    grp = pl.program_id(1)
    n = pl.program_id(2)
    n_sub = q_ref.shape[1] // DIL_PERM
    width = DIL_PERM // dilation
    cur = n & 1
    prev = 1 - cur
    perm = perm_ref[...]

    def deinterleave(x):
        y = jnp.dot(perm, x, preferred_element_type=F32).astype(BF16)
        return y.reshape(dilation, width, x.shape[1])

    for sub in range(n_sub):
        rows = slice(sub * DIL_PERM, (sub + 1) * DIL_PERM)
        qp_ref[:, sub] = deinterleave(q_ref[0, rows, :])
        kp_ref[cur, :, sub] = deinterleave(k_ref[0, rows, :])
        vp_ref[cur, :, sub] = deinterleave(v_ref[0, rows, :])

    @pl.when(n == 0)
    def _():
        kp_ref[prev] = jnp.zeros(kp_ref.shape[1:], BF16)
        vp_ref[prev] = jnp.zeros(vp_ref.shape[1:], BF16)

    slopes = [slopes_ref[grp * DIL_HEAD_GROUP + i] for i in range(DIL_HEAD_GROUP)]
    blk_rows = lambda a: a.reshape(DIL_BLOCK, a.shape[-1])

    def residues(i, carry):
        for r in (2 * i, 2 * i + 1):
            o, lse = _dilated_heads(blk_rows(qp_ref[r]), blk_rows(kp_ref[prev, r]), blk_rows(kp_ref[cur, r]),
                                    blk_rows(vp_ref[prev, r]), blk_rows(vp_ref[cur, r]), slopes, n > 0,
                                    dilation, span)
            op_ref[r] = o.reshape(n_sub, width, DIL_COLS)
            lp_ref[r] = _lse_tile(lse).reshape(n_sub, width, LANE)
        return carry

    lax.fori_loop(0, dilation // 2, residues, 0)

    perm_t = perm_t_ref[...]
    for sub in range(n_sub):
        rows = slice(sub * DIL_PERM, (sub + 1) * DIL_PERM)
        o_sub = op_ref[:, sub].reshape(DIL_PERM, DIL_COLS)
        o_ref[0, rows, :] = jnp.dot(perm_t, o_sub, preferred_element_type=F32).astype(BF16)
        l_sub = lp_ref[:, sub].reshape(DIL_PERM, LANE)
        lse_ref[0, rows, :] = sum(jnp.dot(perm_t, part, preferred_element_type=F32)
                                  for part in _split_bf16(l_sub, 3))


def _contiguous_attention(qkv_a, slopes_a, B, S, window):
    nb = S // DIL_BLOCK
    a3 = qkv_a.reshape(B, S, COLS_MAIN)
    blk = (1, DIL_BLOCK, WIDTH_A)
    prev = lambda n: jnp.maximum(n - 1, 0)
    o, lse = pl.pallas_call(
        functools.partial(_dilated_kernel, dilation=1, span=window),
        grid=(B, 1, nb),
        in_specs=[
            pl.BlockSpec(memory_space=pltpu.SMEM),
            pl.BlockSpec(blk, lambda b, r, n: (b, n, 0)),
            pl.BlockSpec(blk, lambda b, r, n: (b, n, 1)),
            pl.BlockSpec(blk, lambda b, r, n: (b, prev(n), 1)),
            pl.BlockSpec(blk, lambda b, r, n: (b, n, 2)),
            pl.BlockSpec(blk, lambda b, r, n: (b, prev(n), 2)),
        ],
        out_specs=[
            pl.BlockSpec(blk, lambda b, r, n: (b, n, 0)),
            pl.BlockSpec((1, DIL_BLOCK, DIL_LSE_W), lambda b, r, n: (b, n, 0)),
        ],
        out_shape=[
            jax.ShapeDtypeStruct((B, S, WIDTH_A), BF16),
            jax.ShapeDtypeStruct((B, S, DIL_LSE_W), F32),
        ],
        compiler_params=_cparams(("arbitrary", "arbitrary", "arbitrary")),
        name="dilated_attention_d1",
    )(slopes_a, a3, a3, a3, a3, a3)
    return o.reshape(B * S, WIDTH_A), lse.reshape(B * S, DIL_LSE_W)


def _permutation(dilation):
    width = DIL_PERM // dilation
    out_row = jnp.arange(DIL_PERM)
    src = (out_row % width) * dilation + out_row // width
    return (src[:, None] == jnp.arange(DIL_PERM)[None, :]).astype(BF16)


def _strided_attention(qkv_a, slopes_a, B, S, window, dilation):
    rows = DIL_BLOCK * dilation
    assert S % rows == 0 and rows % DIL_PERM == 0 and DIL_PERM % dilation == 0
    assert (DIL_PERM // dilation) % 16 == 0
    assert dilation % 2 == 0
    n_sub, width = rows // DIL_PERM, DIL_PERM // dilation
    n_grp = N_HEADS_A // DIL_HEAD_GROUP
    per = WIDTH_A // DIL_COLS
    a3 = qkv_a.reshape(B, S, COLS_MAIN)
    perm = _permutation(dilation)
    blk = (1, rows, DIL_COLS)
    whole = lambda b, g, n: (0, 0)
    packed = lambda dt: pltpu.VMEM((dilation, n_sub, width, DIL_COLS), dt)
    o, lse = pl.pallas_call(
        functools.partial(_strided_kernel, dilation=dilation, span=window // dilation),
        grid=(B, n_grp, S // rows),
        in_specs=[
            pl.BlockSpec(memory_space=pltpu.SMEM),
            pl.BlockSpec((DIL_PERM, DIL_PERM), whole),
            pl.BlockSpec((DIL_PERM, DIL_PERM), whole),
            pl.BlockSpec(blk, lambda b, g, n: (b, n, g)),
            pl.BlockSpec(blk, lambda b, g, n: (b, n, per + g)),
            pl.BlockSpec(blk, lambda b, g, n: (b, n, 2 * per + g)),
        ],
        out_specs=[
            pl.BlockSpec(blk, lambda b, g, n: (b, n, g)),
            pl.BlockSpec((1, rows, LANE), lambda b, g, n: (b, n, g)),
        ],
        out_shape=[
            jax.ShapeDtypeStruct((B, S, WIDTH_A), BF16),
            jax.ShapeDtypeStruct((B, S, DIL_LSE_W), F32),
        ],
        scratch_shapes=[
            packed(BF16),
            pltpu.VMEM((2, dilation, n_sub, width, DIL_COLS), BF16),
            pltpu.VMEM((2, dilation, n_sub, width, DIL_COLS), BF16),
            packed(BF16),
            pltpu.VMEM((dilation, n_sub, width, LANE), F32),
        ],
        compiler_params=_cparams(("arbitrary", "arbitrary", "arbitrary")),
        name=f"dilated_attention_d{dilation}",
    )(slopes_a, perm, perm.T, a3, a3, a3)
    return o.reshape(B * S, WIDTH_A), lse.reshape(B * S, DIL_LSE_W)


CHUNK_FLAT = CMP_STRIDE * HEAD_DIM


def _compress_kernel(x_ref, pe_ref, w1_ref, w2_ref, o_ref):
    x = x_ref[0, 0].astype(F32)
    first = jnp.dot((x + pe_ref[0:1, :]).astype(BF16), w1_ref[0:CHUNK_FLAT, :],
                    preferred_element_type=F32)
    second = jnp.dot((x + pe_ref[1:2, :]).astype(BF16), w1_ref[CHUNK_FLAT:2 * CHUNK_FLAT, :],
                     preferred_element_type=F32)
    nc = x.shape[0]
    hidden = first + pltpu.roll(second, shift=nc - 1, axis=0)
    act = hidden * jax.nn.sigmoid(hidden)
    o_ref[0, 0] = jnp.dot(act.astype(BF16), w2_ref[...], preferred_element_type=F32).astype(BF16)


def _compress(chunks, pe2, w1, w2):
    B, G, nc, _ = chunks.shape
    return pl.pallas_call(
        _compress_kernel,
        grid=(B, G),
        in_specs=[
            pl.BlockSpec((1, 1, nc, CHUNK_FLAT), lambda b, g: (b, g, 0, 0)),
            pl.BlockSpec((2, CHUNK_FLAT), lambda b, g: (0, 0)),
            pl.BlockSpec((2 * CHUNK_FLAT, CMP_HIDDEN), lambda b, g: (0, 0)),
            pl.BlockSpec((CMP_HIDDEN, HEAD_DIM), lambda b, g: (0, 0)),
        ],
        out_specs=pl.BlockSpec((1, 1, nc, HEAD_DIM), lambda b, g: (b, g, 0, 0)),
        out_shape=jax.ShapeDtypeStruct((B, G, nc, HEAD_DIM), BF16),
        compiler_params=_cparams(("arbitrary", "arbitrary")),
        name="nsa_compress",
    )(chunks, pe2, w1, w2)


NSA_TQ = 256
NSA_TK = 512
WIN_SPAN = WIN_SIZE + NSA_TQ
SLOPE_PIECES = 3


def _stack_heads(a):
    return jnp.concatenate([a[:, h * HEAD_DIM:(h + 1) * HEAD_DIM] for h in range(NSA_HPG)], axis=0)


def _nsa_kernel(slopes_ref, q_ref, kc_ref, vct_ref, ks_ref, vst_ref, kw_ref, vwt_ref, kf_ref, gate_ref,
                ovl_ref, o_ref, *, n_sel):
    g = pl.program_id(1)
    qi = pl.program_id(2)
    t0 = qi * NSA_TQ
    cols = NSA_HPG * NSA_TQ

    q = _stack_heads(q_ref[0])
    slope_row = jnp.concatenate(
        [jnp.full((1, NSA_TQ), slopes_ref[SLOPE_PIECES, g * NSA_HPG + h], F32) for h in range(NSA_HPG)],
        axis=1)
    t_row = t0 + (lax.broadcasted_iota(jnp.int32, (1, cols), 1) & (NSA_TQ - 1))

    n_cmp_pad = kc_ref.shape[2]
    cend = (lax.broadcasted_iota(jnp.int32, (n_cmp_pad, 1), 0) * CMP_STRIDE + (CMP_BLOCK - 1))
    rel_c = t_row - cend
    ok_c = rel_c >= 0
    s = _nt_dot(kc_ref[0, 0], q)
    s = s - slope_row * rel_c.astype(F32)
    s = jnp.where(ok_c, s, NEG_INF)
    m = jnp.max(s, axis=0, keepdims=True)
    e = jnp.where(ok_c, jnp.exp2(s - m), 0.0)
    p = e / jnp.maximum(jnp.sum(e, axis=0, keepdims=True), 1e-30)
    o_cmp = jnp.dot(vct_ref[0, 0], p.astype(BF16), preferred_element_type=F32)

    lane = lax.broadcasted_iota(jnp.int32, (1, LANE), 1)
    slope_feat = []
    for h in range(NSA_HPG):
        row = jnp.zeros((1, LANE), F32)
        for part in range(2):
            for piece in range(SLOPE_PIECES):
                row = jnp.where(lane == n_sel + part * SLOPE_PIECES + piece,
                                slopes_ref[piece, g * NSA_HPG + h], row)
        slope_feat.append(jnp.broadcast_to(row, (NSA_TQ, LANE)))
    slope_feat = jnp.concatenate(slope_feat, axis=0)
    q_win = jnp.concatenate([q, slope_feat.astype(BF16)], axis=1)

    def keys_aug(k_ref, start, size):
        return jnp.concatenate([k_ref[0, pl.ds(start, size), :], kf_ref[pl.ds(start, size), :]], axis=1)

    n_pieces = WIN_SIZE // NSA_TQ + 1
    win_scores, win_starts = [], []
    for i in range(n_pieces):
        first = t0 - WIN_SIZE + i * NSA_TQ
        start = pl.multiple_of(jnp.maximum(first, 0), NSA_TQ)
        s_i = _nt_dot(keys_aug(kw_ref, start, NSA_TQ), q_win)
        rel_i = t_row - (first + lax.broadcasted_iota(jnp.int32, (NSA_TQ, 1), 0))
        if i == n_pieces - 1:
            s_i = jnp.where(rel_i >= 0, s_i, NEG_INF)
        elif i == 0:
            s_i = jnp.where(rel_i < jnp.where(first >= 0, WIN_SIZE, -WIN_SPAN), s_i, NEG_INF)
        else:
            s_i = s_i + jnp.where(first >= 0, 0.0, NEG_INF)
        win_scores.append(s_i)
        win_starts.append(start)
    m_w = functools.reduce(jnp.maximum, [jnp.max(s_i, axis=0, keepdims=True) for s_i in win_scores])
    win_probs = [jnp.exp2(s_i - m_w) for s_i in win_scores]
    l_w = sum(jnp.sum(p_i, axis=0, keepdims=True) for p_i in win_probs)
    o_win = sum(jnp.dot(vwt_ref[0, 0, :, pl.ds(start, NSA_TQ)], p_i.astype(BF16), preferred_element_type=F32)
                for start, p_i in zip(win_starts, win_probs)) / l_w

    p_grp = p[:, 0:NSA_TQ]
    for h in range(1, NSA_HPG):
        p_grp = p_grp + p[:, h * NSA_TQ:(h + 1) * NSA_TQ]
    p_hi = p_grp.astype(BF16)
    p_lo = (p_grp - p_hi.astype(F32)).astype(BF16)
    ovl = ovl_ref[...]
    imp_t = (jnp.dot(ovl, p_hi, preferred_element_type=F32)
             + jnp.dot(ovl, p_lo, preferred_element_type=F32))
    blk = lax.broadcasted_iota(jnp.int32, (n_sel, NSA_TQ), 0)
    cur = (t0 + lax.broadcasted_iota(jnp.int32, (n_sel, NSA_TQ), 1)) // SEL_BLOCK
    valid = blk <= cur
    forced = (blk == 0) | (valid & (blk > cur - SEL_FORCED_LOCAL))
    score = jnp.where(valid, imp_t + jnp.where(forced, FORCE_BONUS, 0.0), -1.0)
    rank = jnp.zeros((n_sel, NSA_TQ), jnp.int32)
    for mth in range(n_sel):
        other = score[mth:mth + 1, :]
        tie_first = jnp.where(blk > mth, 1, 0)
        rank = rank + jnp.where(other > score, 1, jnp.where(other == score, tie_first, 0))
    member_t = jnp.where(valid, jnp.where(rank < SEL_TOP_N, 1.0, 0.0), 0.0).astype(BF16)
    qrow =lax.broadcasted_iota(jnp.int32, (cols, NSA_TQ), 0) & (NSA_TQ - 1)
    eye = jnp.where(qrow == lax.broadcasted_iota(jnp.int32, (cols, NSA_TQ), 1), 1.0, 0.0).astype(BF16)
    member_pad = jnp.concatenate([member_t, jnp.ones((LANE - n_sel, NSA_TQ), BF16)], axis=0)
    picked = _nt_dot(eye, member_pad)
    q_sel = jnp.concatenate([q, (jnp.where(picked > 0.5, 0.0, NEG_INF) + slope_feat).astype(BF16)], axis=1)

    def sel_step(j, carry, causal):
        m_i, l_i, acc = carry
        start = pl.multiple_of(j * NSA_TK, NSA_TK)
        s_ = _nt_dot(keys_aug(ks_ref, start, NSA_TK), q_sel)
        if causal:
            pos = start + lax.broadcasted_iota(jnp.int32, (NSA_TK, 1), 0)
            s_ = jnp.where(pos <= t_row, s_, NEG_INF)
        m_new = jnp.maximum(m_i, jnp.max(s_, axis=0, keepdims=True))
        alpha = jnp.exp2(m_i - m_new)
        p_ = jnp.exp2(s_ - m_new)
        l_new = alpha * l_i + jnp.sum(p_, axis=0, keepdims=True)
        acc_new = alpha * acc + jnp.dot(vst_ref[0, 0, :, pl.ds(start, NSA_TK)], p_.astype(BF16),
                                        preferred_element_type=F32)
        return m_new, l_new, acc_new

    init = (jnp.full((1, cols), NEG_INF, F32), jnp.zeros((1, cols), F32),
            jnp.zeros((HEAD_DIM, cols), F32))
    last = (t0 + NSA_TQ - 1) // NSA_TK
    carry = lax.fori_loop(0, last, functools.partial(sel_step, causal=False), init)
    _, l_s, acc_s = sel_step(last, carry, causal=True)
    o_sel = acc_s / l_s

    gates_t = jax.nn.sigmoid(gate_ref[0]).T
    for h in range(NSA_HPG):
        cs = slice(h * NSA_TQ, (h + 1) * NSA_TQ)
        gsel = [gates_t[h * N_GATES + k:h * N_GATES + k + 1, :] for k in range(N_GATES)]
        mix_t = gsel[0] * o_cmp[:, cs] + gsel[1] * o_sel[:, cs] + gsel[2] * o_win[:, cs]
        o_ref[0, :, h * HEAD_DIM:(h + 1) * HEAD_DIM] = mix_t.T


def _key_features(S):
    pos = jnp.arange(S)[:, None]
    lane = jnp.arange(LANE)[None, :]
    n_sel = S // SEL_BLOCK
    feat = jnp.where(lane == pos // SEL_BLOCK, 1, 0)
    feat = jnp.where((lane >= n_sel) & (lane < n_sel + SLOPE_PIECES), pos % SEL_BLOCK, feat)
    feat = jnp.where((lane >= n_sel + SLOPE_PIECES) & (lane < n_sel + 2 * SLOPE_PIECES),
                     (pos // SEL_BLOCK) * SEL_BLOCK, feat)
    return feat.astype(BF16)


def _slope_table(slopes):
    rows, rest = [], slopes
    for _ in range(SLOPE_PIECES):
        piece = rest.astype(BF16).astype(F32)
        rows.append(piece)
        rest = rest - piece
    return jnp.stack(rows + [slopes])


def _nsa_attention(qkv_b, kc, vc_t, vs_t, vw_t, gates, slopes_b, overlap_t, B, S):
    assert S % NSA_TK == 0 and S >= WIN_SPAN
    assert S // SEL_BLOCK <= SEL_BLOCK and S // SEL_BLOCK + 2 * SLOPE_PIECES <= LANE
    b3 = qkv_b.reshape(B, S, COLS_MAIN)
    g3 = gates.reshape(B, S, GATE_PAD)
    n_cmp_pad = kc.shape[2]
    n_sel = S // SEL_BLOCK
    q_w = NSA_HPG * HEAD_DIM
    q0 = COLS_A // q_w
    kv0 = (COLS_A + WIDTH_B) // HEAD_DIM
    per = KV_WIDTH_B // HEAD_DIM
    seq_blk = (1, S, HEAD_DIM)
    seq_t_blk = (1, 1, HEAD_DIM, S)
    grp = lambda b, g, i: (b, g, 0, 0)
    out = pl.pallas_call(
        functools.partial(_nsa_kernel, n_sel=n_sel),
        grid=(B, NSA_KV_GROUPS, S // NSA_TQ),
        in_specs=[
            pl.BlockSpec(memory_space=pltpu.SMEM),
            pl.BlockSpec((1, NSA_TQ, q_w), lambda b, g, i: (b, i, q0 + g)),
            pl.BlockSpec((1, 1, n_cmp_pad, HEAD_DIM), grp),
            pl.BlockSpec((1, 1, HEAD_DIM, n_cmp_pad), grp),
            pl.BlockSpec(seq_blk, lambda b, g, i: (b, 0, kv0 + 2 * per + g)),
            pl.BlockSpec(seq_t_blk, grp),
            pl.BlockSpec(seq_blk, lambda b, g, i: (b, 0, kv0 + 4 * per + g)),
            pl.BlockSpec(seq_t_blk, grp),
            pl.BlockSpec((S, LANE), lambda b, g, i: (0, 0)),
            pl.BlockSpec((1, NSA_TQ, LANE), lambda b, g, i: (b, i, g)),
            pl.BlockSpec((n_sel, n_cmp_pad), lambda b, g, i: (0, 0)),
        ],
        out_specs=pl.BlockSpec((1, NSA_TQ, q_w), lambda b, g, i: (b, i, g)),
        out_shape=jax.ShapeDtypeStruct((B, S, WIDTH_B), F32),
        compiler_params=_cparams(("arbitrary", "arbitrary", "arbitrary")),
        name="nsa_attention",
    )(_slope_table(slopes_b * LOG2_E), b3, kc, vc_t, b3, vs_t, b3, vw_t, _key_features(S), g3, overlap_t)
    return out.reshape(B * S, WIDTH_B)


OUT_TM = 256


def _rms(x, g):
    return x * lax.rsqrt(jnp.mean(x * x, axis=-1, keepdims=True) + RMS_EPS) * g


def _outproj_kernel(o1_ref, o2_ref, o3_ref, l1_ref, l2_ref, l3_ref, ob_ref, x_ref, ga_ref, gb_ref,
                    w_ref, g2_ref, h_ref, hn_ref):
    l1, l2, l3 = l1_ref[...], l2_ref[...], l3_ref[...]
    mx = jnp.maximum(jnp.maximum(l1, l2), l3)
    e1, e2, e3 = jnp.exp(l1 - mx), jnp.exp(l2 - mx), jnp.exp(l3 - mx)
    den = e1 + e2 + e3
    w1, w2, w3 = e1 / den, e2 / den, e3 / den
    parts = []
    for h in range(N_HEADS_A):
        hs = slice(h * HEAD_DIM, (h + 1) * HEAD_DIM)
        c = (h // DIL_HEAD_GROUP) * LANE + h % DIL_HEAD_GROUP
        parts.append(w1[:, c:c + 1] * o1_ref[:, hs] + w2[:, c:c + 1] * o2_ref[:, hs]
                     + w3[:, c:c + 1] * o3_ref[:, hs])
    o_a = jnp.concatenate(parts, axis=-1)
    mixed = jnp.concatenate([_rms(o_a, ga_ref[...]), _rms(ob_ref[...], gb_ref[...])], axis=-1)
    h = x_ref[...] + jnp.dot(mixed.astype(BF16), w_ref[...], preferred_element_type=F32)
    h_ref[...] = h
    hn_ref[...] = _rms(h, g2_ref[...]).astype(BF16)


def _out_projection(o_as, lses, o_b, x2, ga, gb, w_out, g2):
    T = x2.shape[0]
    row = lambda w: pl.BlockSpec((OUT_TM, w), lambda i: (i, 0))
    full = lambda r, c: pl.BlockSpec((r, c), lambda i: (0, 0))
    return pl.pallas_call(
        _outproj_kernel,
        grid=(T // OUT_TM,),
        in_specs=[row(WIDTH_A)] * 3 + [row(DIL_LSE_W)] * 3 + [row(WIDTH_B), row(D_MODEL),
                  full(1, WIDTH_A), full(1, WIDTH_B), full(WIDTH_A + WIDTH_B, D_MODEL), full(1, D_MODEL)],
        out_specs=[row(D_MODEL), row(D_MODEL)],
        out_shape=[jax.ShapeDtypeStruct((T, D_MODEL), F32), jax.ShapeDtypeStruct((T, D_MODEL), BF16)],
        compiler_params=_cparams(("arbitrary",)),
        name="out_projection",
    )(*o_as, *lses, o_b, x2, ga, gb, w_out, g2)


FFN_TM = 512
FFN_TH = 512


def _ffn_kernel(hn_ref, h_ref, wg_ref, wu_ref, wd_ref, gf_ref, o_ref, acc_ref):
    j = pl.program_id(1)

    @pl.when(j == 0)
    def _():
        acc_ref[...] = jnp.zeros(acc_ref.shape, F32)

    hn = hn_ref[...]
    gate = jnp.dot(hn, wg_ref[...], preferred_element_type=F32)
    up = jnp.dot(hn, wu_ref[...], preferred_element_type=F32)
    act = (gate * jax.nn.sigmoid(gate) * up).astype(BF16)
    acc_ref[...] += jnp.dot(act, wd_ref[...], preferred_element_type=F32)

    @pl.when(j == pl.num_programs(1) - 1)
    def _():
        o_ref[...] = _rms(h_ref[...] + acc_ref[...], gf_ref[...])


def _ffn(hn2, h, w_gate, w_up, w_down, gf):
    T = h.shape[0]
    return pl.pallas_call(
        _ffn_kernel,
        grid=(T // FFN_TM, FFN_HIDDEN // FFN_TH),
        in_specs=[
            pl.BlockSpec((FFN_TM, D_MODEL), lambda i, j: (i, 0)),
            pl.BlockSpec((FFN_TM, D_MODEL), lambda i, j: (i, 0)),
            pl.BlockSpec((D_MODEL, FFN_TH), lambda i, j: (0, j)),
            pl.BlockSpec((D_MODEL, FFN_TH), lambda i, j: (0, j)),
            pl.BlockSpec((FFN_TH, D_MODEL), lambda i, j: (j, 0)),
            pl.BlockSpec((1, D_MODEL), lambda i, j: (0, 0)),
        ],
        out_specs=pl.BlockSpec((FFN_TM, D_MODEL), lambda i, j: (i, 0)),
        out_shape=jax.ShapeDtypeStruct((T, D_MODEL), F32),
        scratch_shapes=[pltpu.VMEM((FFN_TM, D_MODEL), F32)],
        compiler_params=_cparams(("arbitrary", "arbitrary")),
        name="swiglu_ffn",
    )(hn2, h, w_gate, w_up, w_down, gf)


def _chunk_view(qkv_b3, col0):
    B, S, _ = qkv_b3.shape
    a = qkv_b3[:, :, col0:col0 + KV_WIDTH_B]
    a = a.reshape(B, S // CMP_STRIDE, CMP_STRIDE, NSA_KV_GROUPS, HEAD_DIM)
    return a.transpose(0, 3, 1, 2, 4).reshape(B, NSA_KV_GROUPS, S // CMP_STRIDE, CHUNK_FLAT)


def _overlap_t(n_sel, n_cmp_pad):
    cstart = jnp.arange(n_cmp_pad)[None, :] * CMP_STRIDE
    sstart = jnp.arange(n_sel)[:, None] * SEL_BLOCK
    return ((cstart < sstart + SEL_BLOCK) & (cstart + CMP_BLOCK > sstart)).astype(BF16)


def kernel(x, norm1_g, w_in, cmp_pe_k, cmp_w1_k, cmp_w2_k, cmp_pe_v, cmp_w1_v, cmp_w2_v, grp_norm_a,
           grp_norm_b, w_out, norm2_g, w_gate, w_up, w_down, final_g):
    B, S, D = x.shape
    assert D == D_MODEL and S % (DIL_BLOCK * max(d for _, d in DIL_CONFIGS)) == 0
    assert (B * S) % IN_TM == 0 and w_in.shape[0] == 1, "single-layer block only"
    T = B * S
    slopes = jnp.exp2(-8.0 * jnp.arange(1, N_HEADS + 1, dtype=F32) / N_HEADS)
    slopes_a, slopes_b = slopes[0::2], slopes[1::2]
    n_main = COLS_A + COLS_B
    per_group = NSA_HPG * N_GATES

    x2 = x.reshape(T, D)
    w_main = w_in[0][:, :n_main].astype(BF16)
    w_g = w_in[0][:, n_main:].reshape(D, NSA_KV_GROUPS, per_group)
    w_g = jnp.pad(w_g, ((0, 0), (0, 0), (0, LANE - per_group))).reshape(D, GATE_PAD).astype(BF16)
    proj, gates = _in_projection(x2, norm1_g[0][None, :], w_main, w_g)

    dil = [_contiguous_attention(proj, slopes_a, B, S, w) if d == 1
           else _strided_attention(proj, slopes_a, B, S, w, d) for w, d in DIL_CONFIGS]

    b3 = proj.reshape(B, S, COLS_MAIN)
    kv_col = COLS_A + WIDTH_B
    pe2 = lambda pe: pe.reshape(2, CHUNK_FLAT)
    kc = _compress(_chunk_view(b3, kv_col), pe2(cmp_pe_k[0]), cmp_w1_k[0].astype(BF16),
                   cmp_w2_k[0].astype(BF16))
    vc = _compress(_chunk_view(b3, kv_col + KV_WIDTH_B), pe2(cmp_pe_v[0]), cmp_w1_v[0].astype(BF16),
                   cmp_w2_v[0].astype(BF16))
    col_vs = kv_col + 3 * KV_WIDTH_B
    col_vw = kv_col + 5 * KV_WIDTH_B
    seq_t = lambda c0: b3[:, :, c0:c0 + KV_WIDTH_B].reshape(B, S, NSA_KV_GROUPS, HEAD_DIM).transpose(0, 2, 3, 1)
    o_b = _nsa_attention(proj, kc, vc.transpose(0, 1, 3, 2), seq_t(col_vs), seq_t(col_vw), gates, slopes_b,
                         _overlap_t(S // SEL_BLOCK, kc.shape[2]), B, S)

    h, hn2 = _out_projection([o for o, _ in dil], [s for _, s in dil], o_b, x2,
                             grp_norm_a[0][None, :], grp_norm_b[0][None, :],
                             w_out[0].astype(BF16), norm2_g[0][None, :])
    out = _ffn(hn2, h, w_gate[0].astype(BF16), w_up[0].astype(BF16), w_down[0].astype(BF16),
               final_g[None, :])
    return out.reshape(B, S, D)
```

```python
import functools
import math

import jax
import jax.numpy as jnp
from jax import lax
from jax.experimental import pallas as pl
from jax.experimental.pallas import tpu as pltpu

F32 = jnp.float32
BF16 = jnp.bfloat16

D_MODEL = 2048
HEAD_DIM = 128
N_HEADS = 16
N_HEADS_A = 8
N_HEADS_B = 8
WIDTH_A = N_HEADS_A * HEAD_DIM
WIDTH_B = N_HEADS_B * HEAD_DIM
DIL_CONFIGS = ((128, 1), (512, 4), (2048, 16))
DIL_BLOCK = 128
NSA_KV_GROUPS = 2
NSA_HPG = N_HEADS_B // NSA_KV_GROUPS
KV_WIDTH_B = NSA_KV_GROUPS * HEAD_DIM
CMP_BLOCK = 32
CMP_STRIDE = 16
CMP_HIDDEN = 2 * HEAD_DIM
SEL_BLOCK = 64
SEL_TOP_N = 16
SEL_FORCED_LOCAL = 2
FORCE_BONUS = 1.0e3
WIN_SIZE = 512
N_GATES = 3
FFN_HIDDEN = 5632
RMS_EPS = 1e-6
NEG_INF = -1e30
LOG2_E = math.log2(math.e)

COLS_A = 3 * WIDTH_A
COLS_B = WIDTH_B + 6 * KV_WIDTH_B
COLS_MAIN = COLS_A + COLS_B
LANE = 128
GATE_PAD = NSA_KV_GROUPS * LANE

VMEM_LIMIT = 56 * 1024 * 1024


def _cparams(sem):
    return pltpu.CompilerParams(dimension_semantics=sem, vmem_limit_bytes=VMEM_LIMIT)


def _nt_dot(a, b):
    return lax.dot_general(a, b, (((1,), (1,)), ((), ())), preferred_element_type=F32)


IN_TM = 1024
IN_TN = 512
N_TILES_A = COLS_A // IN_TN
N_TILES_B = COLS_B // IN_TN
Q_B_TILE0 = N_TILES_A


def _inproj_kernel(x_ref, g_ref, w_ref, wg_ref, o_ref, og_ref, xn_ref, *, scale):
    j = pl.program_id(1)

    @pl.when(j == 0)
    def _():
        x = x_ref[...]
        y = x * lax.rsqrt(jnp.mean(x * x, axis=-1, keepdims=True) + RMS_EPS)
        xn = (y * g_ref[...]).astype(BF16)
        xn_ref[...] = xn
        og_ref[...] = jnp.dot(xn, wg_ref[...], preferred_element_type=F32)

    acc = jnp.dot(xn_ref[...], w_ref[0], preferred_element_type=F32)
    col_scale = jnp.where(j < 2, scale,
                          jnp.where((j >= Q_B_TILE0) & (j < Q_B_TILE0 + 2), scale * LOG2_E, 1.0))
    o_ref[...] = (acc * col_scale.astype(F32)).astype(BF16)


def _in_projection(x2, g, w_main, w_gate):
    T = x2.shape[0]
    grid = (T // IN_TM, N_TILES_A + N_TILES_B)
    return pl.pallas_call(
        functools.partial(_inproj_kernel, scale=1.0 / math.sqrt(HEAD_DIM)),
        grid=grid,
        in_specs=[
            pl.BlockSpec((IN_TM, D_MODEL), lambda i, j: (i, 0)),
            pl.BlockSpec((1, D_MODEL), lambda i, j: (0, 0)),
            pl.BlockSpec((1, D_MODEL, IN_TN), lambda i, j: (j, 0, 0)),
            pl.BlockSpec((D_MODEL, GATE_PAD), lambda i, j: (0, 0)),
        ],
        out_specs=[
            pl.BlockSpec((IN_TM, IN_TN), lambda i, j: (i, j)),
            pl.BlockSpec((IN_TM, GATE_PAD), lambda i, j: (i, 0)),
        ],
        out_shape=[
            jax.ShapeDtypeStruct((T, COLS_MAIN), BF16),
            jax.ShapeDtypeStruct((T, GATE_PAD), F32),
        ],
        scratch_shapes=[pltpu.VMEM((IN_TM, D_MODEL), BF16)],
        compiler_params=_cparams(("arbitrary", "arbitrary")),
        name="in_projection",
    )(x2, g, w_main, w_gate)


DIL_HEAD_GROUP = 4
DIL_COLS = DIL_HEAD_GROUP * HEAD_DIM
DIL_LSE_W = (N_HEADS_A // DIL_HEAD_GROUP) * LANE
DIL_PERM = 256


def _dilated_heads(q, k_prev, k_cur, v_prev, v_cur, slopes, has_prev, dilation, span):
    key = lax.broadcasted_iota(jnp.int32, (2 * DIL_BLOCK, DIL_BLOCK), 0)
    qry = lax.broadcasted_iota(jnp.int32, (2 * DIL_BLOCK, DIL_BLOCK), 1)
    rel = qry + DIL_BLOCK - key
    in_prev = jnp.where(rel <= span, jnp.where(has_prev, 0.0, NEG_INF), NEG_INF)
    mask_bias = jnp.where(key < DIL_BLOCK, in_prev, jnp.where(rel >= 0, 0.0, NEG_INF))
    dist = (rel * dilation).astype(F32)
    heads = [slice(h * HEAD_DIM, (h + 1) * HEAD_DIM) for h in range(len(slopes))]
    scores = [_nt_dot(jnp.concatenate([k_prev[:, hs], k_cur[:, hs]], axis=0), q[:, hs]) for hs in heads]
    scores = [s - slope * dist + mask_bias for slope, s in zip(slopes, scores)]
    maxes = [jnp.max(s, axis=0, keepdims=True) for s in scores]
    probs = [jnp.exp(s - m) for s, m in zip(scores, maxes)]
    sums = [jnp.sum(p, axis=0, keepdims=True) for p in probs]
    outs = [lax.dot_general((p * (1.0 / l)).astype(BF16),
                            jnp.concatenate([v_prev[:, hs], v_cur[:, hs]], axis=0),
                            (((0,), (0,)), ((), ())), preferred_element_type=F32)
            for hs, p, l in zip(heads, probs, sums)]
    return (jnp.concatenate(outs, axis=1).astype(BF16),
            [m + jnp.log(l) for m, l in zip(maxes, sums)])


def _lse_tile(rows):
    pad = jnp.zeros((LANE - len(rows), DIL_BLOCK), F32)
    return jnp.concatenate(list(rows) + [pad], axis=0).T


def _dilated_kernel(slopes_ref, q_ref, kc_ref, kp_ref, vc_ref, vp_ref, o_ref, lse_ref, *,
                    dilation, span):
    slopes = [slopes_ref[h] for h in range(N_HEADS_A)]
    o, lse = _dilated_heads(q_ref[0], kp_ref[0], kc_ref[0], vp_ref[0], vc_ref[0], slopes,
                            pl.program_id(2) > 0, dilation, span)
    o_ref[0] = o
    for grp in range(N_HEADS_A // DIL_HEAD_GROUP):
        lse_ref[0, :, grp * LANE:(grp + 1) * LANE] = _lse_tile(
            lse[grp * DIL_HEAD_GROUP:(grp + 1) * DIL_HEAD_GROUP])


def _split_bf16(x, pieces):
    out, rest = [], x
    for _ in range(pieces):
        part = rest.astype(BF16)
        out.append(part)
        rest = rest - part.astype(F32)
    return out


def _strided_kernel(slopes_ref, perm_ref, perm_t_ref, q_ref, k_ref, v_ref, o_ref, lse_ref,
                    qp_ref, kp_ref, vp_ref, op_ref, lp_ref, *, dilation, span):
    grp = pl.program_id(1)
    n = pl.program_id(2)
    n_sub = q_ref.shape[1] // DIL_PERM
    width = DIL_PERM // dilation
    cur = n & 1
    prev = 1 - cur
    perm = perm_ref[...]

    def deinterleave(x):
        y = jnp.dot(perm, x, preferred_element_type=F32).astype(BF16)
        return y.reshape(dilation, width, x.shape[1])

    for sub in range(n_sub):
        rows = slice(sub * DIL_PERM, (sub + 1) * DIL_PERM)
        qp_ref[:, sub] = deinterleave(q_ref[0, rows, :])
        kp_ref[cur, :, sub] = deinterleave(k_ref[0, rows, :])
        vp_ref[cur, :, sub] = deinterleave(v_ref[0, rows, :])

    @pl.when(n == 0)
    def _():
        kp_ref[prev] = jnp.zeros(kp_ref.shape[1:], BF16)
        vp_ref[prev] = jnp.zeros(vp_ref.shape[1:], BF16)

    slopes = [slopes_ref[grp * DIL_HEAD_GROUP + i] for i in range(DIL_HEAD_GROUP)]
    blk_rows = lambda a: a.reshape(DIL_BLOCK, a.shape[-1])

    def residues(i, carry):
        for r in (2 * i, 2 * i + 1):
            o, lse = _dilated_heads(blk_rows(qp_ref[r]), blk_rows(kp_ref[prev, r]), blk_rows(kp_ref[cur, r]),
                                    blk_rows(vp_ref[prev, r]), blk_rows(vp_ref[cur, r]), slopes, n > 0,
                                    dilation, span)
            op_ref[r] = o.reshape(n_sub, width, DIL_COLS)
            lp_ref[r] = _lse_tile(lse).reshape(n_sub, width, LANE)
        return carry

    lax.fori_loop(0, dilation // 2, residues, 0)

    perm_t = perm_t_ref[...]
    for sub in range(n_sub):
        rows = slice(sub * DIL_PERM, (sub + 1) * DIL_PERM)
        o_sub = op_ref[:, sub].reshape(DIL_PERM, DIL_COLS)
        o_ref[0, rows, :] = jnp.dot(perm_t, o_sub, preferred_element_type=F32).astype(BF16)
        l_sub = lp_ref[:, sub].reshape(DIL_PERM, LANE)
        lse_ref[0, rows, :] = sum(jnp.dot(perm_t, part, preferred_element_type=F32)
                                  for part in _split_bf16(l_sub, 3))


def _contiguous_attention(qkv_a, slopes_a, B, S, window):
    nb = S // DIL_BLOCK
    a3 = qkv_a.reshape(B, S, COLS_MAIN)
    blk = (1, DIL_BLOCK, WIDTH_A)
    prev = lambda n: jnp.maximum(n - 1, 0)
    o, lse = pl.pallas_call(
        functools.partial(_dilated_kernel, dilation=1, span=window),
        grid=(B, 1, nb),
        in_specs=[
            pl.BlockSpec(memory_space=pltpu.SMEM),
            pl.BlockSpec(blk, lambda b, r, n: (b, n, 0)),
            pl.BlockSpec(blk, lambda b, r, n: (b, n, 1)),
            pl.BlockSpec(blk, lambda b, r, n: (b, prev(n), 1)),
            pl.BlockSpec(blk, lambda b, r, n: (b, n, 2)),
            pl.BlockSpec(blk, lambda b, r, n: (b, prev(n), 2)),
        ],
        out_specs=[
            pl.BlockSpec(blk, lambda b, r, n: (b, n, 0)),
            pl.BlockSpec((1, DIL_BLOCK, DIL_LSE_W), lambda b, r, n: (b, n, 0)),
        ],
        out_shape=[
            jax.ShapeDtypeStruct((B, S, WIDTH_A), BF16),
            jax.ShapeDtypeStruct((B, S, DIL_LSE_W), F32),
        ],
        compiler_params=_cparams(("arbitrary", "arbitrary", "arbitrary")),
        name="dilated_attention_d1",
    )(slopes_a, a3, a3, a3, a3, a3)
    return o.reshape(B * S, WIDTH_A), lse.reshape(B * S, DIL_LSE_W)


def _permutation(dilation):
    width = DIL_PERM // dilation
    out_row = jnp.arange(DIL_PERM)
    src = (out_row % width) * dilation + out_row // width
    return (src[:, None] == jnp.arange(DIL_PERM)[None, :]).astype(BF16)


def _strided_attention(qkv_a, slopes_a, B, S, window, dilation):
    rows = DIL_BLOCK * dilation
    assert S % rows == 0 and rows % DIL_PERM == 0 and DIL_PERM % dilation == 0
    assert (DIL_PERM // dilation) % 16 == 0
    assert dilation % 2 == 0
    n_sub, width = rows // DIL_PERM, DIL_PERM // dilation
    n_grp = N_HEADS_A // DIL_HEAD_GROUP
    per = WIDTH_A // DIL_COLS
    a3 = qkv_a.reshape(B, S, COLS_MAIN)
    perm = _permutation(dilation)
    blk = (1, rows, DIL_COLS)
    whole = lambda b, g, n: (0, 0)
    packed = lambda dt: pltpu.VMEM((dilation, n_sub, width, DIL_COLS), dt)
    o, lse = pl.pallas_call(
        functools.partial(_strided_kernel, dilation=dilation, span=window // dilation),
        grid=(B, n_grp, S // rows),
        in_specs=[
            pl.BlockSpec(memory_space=pltpu.SMEM),
            pl.BlockSpec((DIL_PERM, DIL_PERM), whole),
            pl.BlockSpec((DIL_PERM, DIL_PERM), whole),
            pl.BlockSpec(blk, lambda b, g, n: (b, n, g)),
            pl.BlockSpec(blk, lambda b, g, n: (b, n, per + g)),
            pl.BlockSpec(blk, lambda b, g, n: (b, n, 2 * per + g)),
        ],
        out_specs=[
            pl.BlockSpec(blk, lambda b, g, n: (b, n, g)),
            pl.BlockSpec((1, rows, LANE), lambda b, g, n: (b, n, g)),
        ],
        out_shape=[
            jax.ShapeDtypeStruct((B, S, WIDTH_A), BF16),
            jax.ShapeDtypeStruct((B, S, DIL_LSE_W), F32),
        ],
        scratch_shapes=[
            packed(BF16),
            pltpu.VMEM((2, dilation, n_sub, width, DIL_COLS), BF16),
            pltpu.VMEM((2, dilation, n_sub, width, DIL_COLS), BF16),
            packed(BF16),
            pltpu.VMEM((dilation, n_sub, width, LANE), F32),
        ],
        compiler_params=_cparams(("arbitrary", "arbitrary", "arbitrary")),
        name=f"dilated_attention_d{dilation}",
    )(slopes_a, perm, perm.T, a3, a3, a3)
    return o.reshape(B * S, WIDTH_A), lse.reshape(B * S, DIL_LSE_W)


CHUNK_FLAT = CMP_STRIDE * HEAD_DIM


def _compress_kernel(x_ref, pe_ref, w1_ref, w2_ref, o_ref):
    x = x_ref[0, 0].astype(F32)
    first = jnp.dot((x + pe_ref[0:1, :]).astype(BF16), w1_ref[0:CHUNK_FLAT, :],
                    preferred_element_type=F32)
    second = jnp.dot((x + pe_ref[1:2, :]).astype(BF16), w1_ref[CHUNK_FLAT:2 * CHUNK_FLAT, :],
                     preferred_element_type=F32)
    nc = x.shape[0]
    hidden = first + pltpu.roll(second, shift=nc - 1, axis=0)
    act = hidden * jax.nn.sigmoid(hidden)
    o_ref[0, 0] = jnp.dot(act.astype(BF16), w2_ref[...], preferred_element_type=F32).astype(BF16)


def _compress(chunks, pe2, w1, w2):
    B, G, nc, _ = chunks.shape
    return pl.pallas_call(
        _compress_kernel,
        grid=(B, G),
        in_specs=[
            pl.BlockSpec((1, 1, nc, CHUNK_FLAT), lambda b, g: (b, g, 0, 0)),
            pl.BlockSpec((2, CHUNK_FLAT), lambda b, g: (0, 0)),
            pl.BlockSpec((2 * CHUNK_FLAT, CMP_HIDDEN), lambda b, g: (0, 0)),
            pl.BlockSpec((CMP_HIDDEN, HEAD_DIM), lambda b, g: (0, 0)),
        ],
        out_specs=pl.BlockSpec((1, 1, nc, HEAD_DIM), lambda b, g: (b, g, 0, 0)),
        out_shape=jax.ShapeDtypeStruct((B, G, nc, HEAD_DIM), BF16),
        compiler_params=_cparams(("arbitrary", "arbitrary")),
        name="nsa_compress",
    )(chunks, pe2, w1, w2)


NSA_TQ = 256
NSA_TK = 512
WIN_SPAN = WIN_SIZE + NSA_TQ
SLOPE_PIECES = 3


def _stack_heads(a):
    return jnp.concatenate([a[:, h * HEAD_DIM:(h + 1) * HEAD_DIM] for h in range(NSA_HPG)], axis=0)


def _nsa_kernel(slopes_ref, q_ref, kc_ref, vct_ref, ks_ref, vst_ref, kw_ref, vwt_ref, kf_ref, gate_ref,
                ovl_ref, o_ref, *, n_sel):
    g = pl.program_id(1)
    qi = pl.program_id(2)
    t0 = qi * NSA_TQ
    cols = NSA_HPG * NSA_TQ

    q = _stack_heads(q_ref[0])
    slope_row = jnp.concatenate(
        [jnp.full((1, NSA_TQ), slopes_ref[SLOPE_PIECES, g * NSA_HPG + h], F32) for h in range(NSA_HPG)],
        axis=1)
    t_row = t0 + (lax.broadcasted_iota(jnp.int32, (1, cols), 1) & (NSA_TQ - 1))

    n_cmp_pad = kc_ref.shape[2]
    cend = (lax.broadcasted_iota(jnp.int32, (n_cmp_pad, 1), 0) * CMP_STRIDE + (CMP_BLOCK - 1))
    rel_c = t_row - cend
    ok_c = rel_c >= 0
    s = _nt_dot(kc_ref[0, 0], q)
    s = s - slope_row * rel_c.astype(F32)
    s = jnp.where(ok_c, s, NEG_INF)
    m = jnp.max(s, axis=0, keepdims=True)
    e = jnp.where(ok_c, jnp.exp2(s - m), 0.0)
    p = e / jnp.maximum(jnp.sum(e, axis=0, keepdims=True), 1e-30)
    o_cmp = jnp.dot(vct_ref[0, 0], p.astype(BF16), preferred_element_type=F32)

    lane = lax.broadcasted_iota(jnp.int32, (1, LANE), 1)
    slope_feat = []
    for h in range(NSA_HPG):
        row = jnp.zeros((1, LANE), F32)
        for part in range(2):
            for piece in range(SLOPE_PIECES):
                row = jnp.where(lane == n_sel + part * SLOPE_PIECES + piece,
                                slopes_ref[piece, g * NSA_HPG + h], row)
        slope_feat.append(jnp.broadcast_to(row, (NSA_TQ, LANE)))
    slope_feat = jnp.concatenate(slope_feat, axis=0)
    q_win = jnp.concatenate([q, slope_feat.astype(BF16)], axis=1)

    def keys_aug(k_ref, start, size):
        return jnp.concatenate([k_ref[0, pl.ds(start, size), :], kf_ref[pl.ds(start, size), :]], axis=1)

    n_pieces = WIN_SIZE // NSA_TQ + 1
    win_scores, win_starts = [], []
    for i in range(n_pieces):
        first = t0 - WIN_SIZE + i * NSA_TQ
        start = pl.multiple_of(jnp.maximum(first, 0), NSA_TQ)
        s_i = _nt_dot(keys_aug(kw_ref, start, NSA_TQ), q_win)
        rel_i = t_row - (first + lax.broadcasted_iota(jnp.int32, (NSA_TQ, 1), 0))
        if i == n_pieces - 1:
            s_i = jnp.where(rel_i >= 0, s_i, NEG_INF)
        elif i == 0:
            s_i = jnp.where(rel_i < jnp.where(first >= 0, WIN_SIZE, -WIN_SPAN), s_i, NEG_INF)
        else:
            s_i = s_i + jnp.where(first >= 0, 0.0, NEG_INF)
        win_scores.append(s_i)
        win_starts.append(start)
    m_w = functools.reduce(jnp.maximum, [jnp.max(s_i, axis=0, keepdims=True) for s_i in win_scores])
    win_probs = [jnp.exp2(s_i - m_w) for s_i in win_scores]
    l_w = sum(jnp.sum(p_i, axis=0, keepdims=True) for p_i in win_probs)
    o_win = sum(jnp.dot(vwt_ref[0, 0, :, pl.ds(start, NSA_TQ)], p_i.astype(BF16), preferred_element_type=F32)
                for start, p_i in zip(win_starts, win_probs)) / l_w

    p_grp = p[:, 0:NSA_TQ]
    for h in range(1, NSA_HPG):
        p_grp = p_grp + p[:, h * NSA_TQ:(h + 1) * NSA_TQ]
    p_hi = p_grp.astype(BF16)
    p_lo = (p_grp - p_hi.astype(F32)).astype(BF16)
    ovl = ovl_ref[...]
    imp_t = (jnp.dot(ovl, p_hi, preferred_element_type=F32)
             + jnp.dot(ovl, p_lo, preferred_element_type=F32))
    blk = lax.broadcasted_iota(jnp.int32, (n_sel, NSA_TQ), 0)
    cur = (t0 + lax.broadcasted_iota(jnp.int32, (n_sel, NSA_TQ), 1)) // SEL_BLOCK
    valid = blk <= cur
    forced = (blk == 0) | (valid & (blk > cur - SEL_FORCED_LOCAL))
    score = jnp.where(valid, imp_t + jnp.where(forced, FORCE_BONUS, 0.0), -1.0)
    rank = jnp.zeros((n_sel, NSA_TQ), jnp.int32)
    for mth in range(n_sel):
        other = score[mth:mth + 1, :]
        tie_first = jnp.where(blk > mth, 1, 0)
        rank = rank + jnp.where(other > score, 1, jnp.where(other == score, tie_first, 0))
    member_t = jnp.where(valid, jnp.where(rank < SEL_TOP_N, 1.0, 0.0), 0.0).astype(BF16)
    qrow =lax.broadcasted_iota(jnp.int32, (cols, NSA_TQ), 0) & (NSA_TQ - 1)
    eye = jnp.where(qrow == lax.broadcasted_iota(jnp.int32, (cols, NSA_TQ), 1), 1.0, 0.0).astype(BF16)
    member_pad = jnp.concatenate([member_t, jnp.ones((LANE - n_sel, NSA_TQ), BF16)], axis=0)
    picked = _nt_dot(eye, member_pad)
    q_sel = jnp.concatenate([q, (jnp.where(picked > 0.5, 0.0, NEG_INF) + slope_feat).astype(BF16)], axis=1)

    def sel_step(j, carry, causal):
        m_i, l_i, acc = carry
        start = pl.multiple_of(j * NSA_TK, NSA_TK)
        s_ = _nt_dot(keys_aug(ks_ref, start, NSA_TK), q_sel)
        if causal:
            pos = start + lax.broadcasted_iota(jnp.int32, (NSA_TK, 1), 0)
            s_ = jnp.where(pos <= t_row, s_, NEG_INF)
        m_new = jnp.maximum(m_i, jnp.max(s_, axis=0, keepdims=True))
        alpha = jnp.exp2(m_i - m_new)
        p_ = jnp.exp2(s_ - m_new)
        l_new = alpha * l_i + jnp.sum(p_, axis=0, keepdims=True)
        acc_new = alpha * acc + jnp.dot(vst_ref[0, 0, :, pl.ds(start, NSA_TK)], p_.astype(BF16),
                                        preferred_element_type=F32)
        return m_new, l_new, acc_new

    init = (jnp.full((1, cols), NEG_INF, F32), jnp.zeros((1, cols), F32),
            jnp.zeros((HEAD_DIM, cols), F32))
    last = (t0 + NSA_TQ - 1) // NSA_TK
    carry = lax.fori_loop(0, last, functools.partial(sel_step, causal=False), init)
    _, l_s, acc_s = sel_step(last, carry, causal=True)
    o_sel = acc_s / l_s

    gates_t = jax.nn.sigmoid(gate_ref[0]).T
    for h in range(NSA_HPG):
        cs = slice(h * NSA_TQ, (h + 1) * NSA_TQ)
        gsel = [gates_t[h * N_GATES + k:h * N_GATES + k + 1, :] for k in range(N_GATES)]
        mix_t = gsel[0] * o_cmp[:, cs] + gsel[1] * o_sel[:, cs] + gsel[2] * o_win[:, cs]
        o_ref[0, :, h * HEAD_DIM:(h + 1) * HEAD_DIM] = mix_t.T


def _key_features(S):
    pos = jnp.arange(S)[:, None]
    lane = jnp.arange(LANE)[None, :]
    n_sel = S // SEL_BLOCK
    feat = jnp.where(lane == pos // SEL_BLOCK, 1, 0)
    feat = jnp.where((lane >= n_sel) & (lane < n_sel + SLOPE_PIECES), pos % SEL_BLOCK, feat)
    feat = jnp.where((lane >= n_sel + SLOPE_PIECES) & (lane < n_sel + 2 * SLOPE_PIECES),
                     (pos // SEL_BLOCK) * SEL_BLOCK, feat)
    return feat.astype(BF16)


def _slope_table(slopes):
    rows, rest = [], slopes
    for _ in range(SLOPE_PIECES):
        piece = rest.astype(BF16).astype(F32)
        rows.append(piece)
        rest = rest - piece
    return jnp.stack(rows + [slopes])


def _nsa_attention(qkv_b, kc, vc_t, vs_t, vw_t, gates, slopes_b, overlap_t, B, S):
    assert S % NSA_TK == 0 and S >= WIN_SPAN
    assert S // SEL_BLOCK <= SEL_BLOCK and S // SEL_BLOCK + 2 * SLOPE_PIECES <= LANE
    b3 = qkv_b.reshape(B, S, COLS_MAIN)
    g3 = gates.reshape(B, S, GATE_PAD)
    n_cmp_pad = kc.shape[2]
    n_sel = S // SEL_BLOCK
    q_w = NSA_HPG * HEAD_DIM
    q0 = COLS_A // q_w
    kv0 = (COLS_A + WIDTH_B) // HEAD_DIM
    per = KV_WIDTH_B // HEAD_DIM
    seq_blk = (1, S, HEAD_DIM)
    seq_t_blk = (1, 1, HEAD_DIM, S)
    grp = lambda b, g, i: (b, g, 0, 0)
    out = pl.pallas_call(
        functools.partial(_nsa_kernel, n_sel=n_sel),
        grid=(B, NSA_KV_GROUPS, S // NSA_TQ),
        in_specs=[
            pl.BlockSpec(memory_space=pltpu.SMEM),
            pl.BlockSpec((1, NSA_TQ, q_w), lambda b, g, i: (b, i, q0 + g)),
            pl.BlockSpec((1, 1, n_cmp_pad, HEAD_DIM), grp),
            pl.BlockSpec((1, 1, HEAD_DIM, n_cmp_pad), grp),
            pl.BlockSpec(seq_blk, lambda b, g, i: (b, 0, kv0 + 2 * per + g)),
            pl.BlockSpec(seq_t_blk, grp),
            pl.BlockSpec(seq_blk, lambda b, g, i: (b, 0, kv0 + 4 * per + g)),
            pl.BlockSpec(seq_t_blk, grp),
            pl.BlockSpec((S, LANE), lambda b, g, i: (0, 0)),
            pl.BlockSpec((1, NSA_TQ, LANE), lambda b, g, i: (b, i, g)),
            pl.BlockSpec((n_sel, n_cmp_pad), lambda b, g, i: (0, 0)),
        ],
        out_specs=pl.BlockSpec((1, NSA_TQ, q_w), lambda b, g, i: (b, i, g)),
        out_shape=jax.ShapeDtypeStruct((B, S, WIDTH_B), F32),
        compiler_params=_cparams(("arbitrary", "arbitrary", "arbitrary")),
        name="nsa_attention",
    )(_slope_table(slopes_b * LOG2_E), b3, kc, vc_t, b3, vs_t, b3, vw_t, _key_features(S), g3, overlap_t)
    return out.reshape(B * S, WIDTH_B)


OUT_TM = 512
OUT_SUB = 256


def _rms(x, g):
    return x * lax.rsqrt(jnp.mean(x * x, axis=-1, keepdims=True) + RMS_EPS) * g


def _outproj_kernel(o1_ref, o2_ref, o3_ref, l1_ref, l2_ref, l3_ref, ob_ref, x_ref, ga_ref, gb_ref,
                    w_ref, g2_ref, h_ref, hn_ref):
    for sub in range(OUT_TM // OUT_SUB):
        rows = slice(sub * OUT_SUB, (sub + 1) * OUT_SUB)
        l1, l2, l3 = l1_ref[rows, :], l2_ref[rows, :], l3_ref[rows, :]
        mx = jnp.maximum(jnp.maximum(l1, l2), l3)
        e1, e2, e3 = jnp.exp(l1 - mx), jnp.exp(l2 - mx), jnp.exp(l3 - mx)
        den = e1 + e2 + e3
        w1, w2, w3 = e1 / den, e2 / den, e3 / den
        parts = []
        for h in range(N_HEADS_A):
            hs = slice(h * HEAD_DIM, (h + 1) * HEAD_DIM)
            c = (h // DIL_HEAD_GROUP) * LANE + h % DIL_HEAD_GROUP
            parts.append(w1[:, c:c + 1] * o1_ref[rows, hs] + w2[:, c:c + 1] * o2_ref[rows, hs]
                         + w3[:, c:c + 1] * o3_ref[rows, hs])
        o_a = jnp.concatenate(parts, axis=-1)
        mixed = jnp.concatenate([_rms(o_a, ga_ref[...]), _rms(ob_ref[rows, :], gb_ref[...])], axis=-1)
        h = x_ref[rows, :] + jnp.dot(mixed.astype(BF16), w_ref[...], preferred_element_type=F32)
        h_ref[rows, :] = h
        hn_ref[rows, :] = _rms(h, g2_ref[...]).astype(BF16)


def _out_projection(o_as, lses, o_b, x2, ga, gb, w_out, g2):
    T = x2.shape[0]
    row = lambda w: pl.BlockSpec((OUT_TM, w), lambda i: (i, 0))
    full = lambda r, c: pl.BlockSpec((r, c), lambda i: (0, 0))
    return pl.pallas_call(
        _outproj_kernel,
        grid=(T // OUT_TM,),
        in_specs=[row(WIDTH_A)] * 3 + [row(DIL_LSE_W)] * 3 + [row(WIDTH_B), row(D_MODEL),
                  full(1, WIDTH_A), full(1, WIDTH_B), full(WIDTH_A + WIDTH_B, D_MODEL), full(1, D_MODEL)],
        out_specs=[row(D_MODEL), row(D_MODEL)],
        out_shape=[jax.ShapeDtypeStruct((T, D_MODEL), F32), jax.ShapeDtypeStruct((T, D_MODEL), BF16)],
        compiler_params=_cparams(("arbitrary",)),
        name="out_projection",
    )(*o_as, *lses, o_b, x2, ga, gb, w_out, g2)


FFN_TM = 512
FFN_TH = 512


def _ffn_kernel(hn_ref, h_ref, wg_ref, wu_ref, wd_ref, gf_ref, o_ref, acc_ref):
    j = pl.program_id(1)

    @pl.when(j == 0)
    def _():
        acc_ref[...] = jnp.zeros(acc_ref.shape, F32)

    hn = hn_ref[...]
    gate = jnp.dot(hn, wg_ref[0], preferred_element_type=F32)
    up = jnp.dot(hn, wu_ref[0], preferred_element_type=F32)
    act = (gate * jax.nn.sigmoid(gate) * up).astype(BF16)
    acc_ref[...] += jnp.dot(act, wd_ref[...], preferred_element_type=F32)

    @pl.when(j == pl.num_programs(1) - 1)
    def _():
        o_ref[...] = _rms(h_ref[...] + acc_ref[...], gf_ref[...])


def _ffn(hn2, h, w_gate, w_up, w_down, gf):
    T = h.shape[0]
    return pl.pallas_call(
        _ffn_kernel,
        grid=(T // FFN_TM, FFN_HIDDEN // FFN_TH),
        in_specs=[
            pl.BlockSpec((FFN_TM, D_MODEL), lambda i, j: (i, 0)),
            pl.BlockSpec((FFN_TM, D_MODEL), lambda i, j: (i, 0)),
            pl.BlockSpec((1, D_MODEL, FFN_TH), lambda i, j: (j, 0, 0)),
            pl.BlockSpec((1, D_MODEL, FFN_TH), lambda i, j: (j, 0, 0)),
            pl.BlockSpec((FFN_TH, D_MODEL), lambda i, j: (j, 0)),
            pl.BlockSpec((1, D_MODEL), lambda i, j: (0, 0)),
        ],
        out_specs=pl.BlockSpec((FFN_TM, D_MODEL), lambda i, j: (i, 0)),
        out_shape=jax.ShapeDtypeStruct((T, D_MODEL), F32),
        scratch_shapes=[pltpu.VMEM((FFN_TM, D_MODEL), F32)],
        compiler_params=_cparams(("arbitrary", "arbitrary")),
        name="swiglu_ffn",
    )(hn2, h, w_gate, w_up, w_down, gf)


def _col_tiles(w, tn):
    k, n = w.shape
    return w.reshape(k, n // tn, tn).transpose(1, 0, 2)


def _chunk_view(qkv_b3, col0):
    B, S, _ = qkv_b3.shape
    a = qkv_b3[:, :, col0:col0 + KV_WIDTH_B]
    a = a.reshape(B, S // CMP_STRIDE, CMP_STRIDE, NSA_KV_GROUPS, HEAD_DIM)
    return a.transpose(0, 3, 1, 2, 4).reshape(B, NSA_KV_GROUPS, S // CMP_STRIDE, CHUNK_FLAT)


def _overlap_t(n_sel, n_cmp_pad):
    cstart = jnp.arange(n_cmp_pad)[None, :] * CMP_STRIDE
    sstart = jnp.arange(n_sel)[:, None] * SEL_BLOCK
    return ((cstart < sstart + SEL_BLOCK) & (cstart + CMP_BLOCK > sstart)).astype(BF16)


def kernel(x, norm1_g, w_in, cmp_pe_k, cmp_w1_k, cmp_w2_k, cmp_pe_v, cmp_w1_v, cmp_w2_v, grp_norm_a,
           grp_norm_b, w_out, norm2_g, w_gate, w_up, w_down, final_g):
    B, S, D = x.shape
    assert D == D_MODEL and S % (DIL_BLOCK * max(d for _, d in DIL_CONFIGS)) == 0
    assert (B * S) % IN_TM == 0 and w_in.shape[0] == 1, "single-layer block only"
    T = B * S
    slopes = jnp.exp2(-8.0 * jnp.arange(1, N_HEADS + 1, dtype=F32) / N_HEADS)
    slopes_a, slopes_b = slopes[0::2], slopes[1::2]
    n_main = COLS_A + COLS_B
    per_group = NSA_HPG * N_GATES

    x2 = x.reshape(T, D)
    w_main = _col_tiles(w_in[0][:, :n_main].astype(BF16), IN_TN)
    w_g = w_in[0][:, n_main:].reshape(D, NSA_KV_GROUPS, per_group)
    w_g = jnp.pad(w_g, ((0, 0), (0, 0), (0, LANE - per_group))).reshape(D, GATE_PAD).astype(BF16)
    proj, gates = _in_projection(x2, norm1_g[0][None, :], w_main, w_g)

    dil = [_contiguous_attention(proj, slopes_a, B, S, w) if d == 1
           else _strided_attention(proj, slopes_a, B, S, w, d) for w, d in DIL_CONFIGS]

    b3 = proj.reshape(B, S, COLS_MAIN)
    kv_col = COLS_A + WIDTH_B
    pe2 = lambda pe: pe.reshape(2, CHUNK_FLAT)
    kc = _compress(_chunk_view(b3, kv_col), pe2(cmp_pe_k[0]), cmp_w1_k[0].astype(BF16),
                   cmp_w2_k[0].astype(BF16))
    vc = _compress(_chunk_view(b3, kv_col + KV_WIDTH_B), pe2(cmp_pe_v[0]), cmp_w1_v[0].astype(BF16),
                   cmp_w2_v[0].astype(BF16))
    col_vs = kv_col + 3 * KV_WIDTH_B
    col_vw = kv_col + 5 * KV_WIDTH_B
    seq_t = lambda c0: b3[:, :, c0:c0 + KV_WIDTH_B].reshape(B, S, NSA_KV_GROUPS, HEAD_DIM).transpose(0, 2, 3, 1)
    o_b = _nsa_attention(proj, kc, vc.transpose(0, 1, 3, 2), seq_t(col_vs), seq_t(col_vw), gates, slopes_b,
                         _overlap_t(S // SEL_BLOCK, kc.shape[2]), B, S)

    h, hn2 = _out_projection([o for o, _ in dil], [s for _, s in dil], o_b, x2,
                             grp_norm_a[0][None, :], grp_norm_b[0][None, :],
                             w_out[0].astype(BF16), norm2_g[0][None, :])
    out = _ffn(hn2, h, _col_tiles(w_gate[0].astype(BF16), FFN_TH), _col_tiles(w_up[0].astype(BF16), FFN_TH), w_down[0].astype(BF16),
               final_g[None, :])
    return out.reshape(B, S, D)
```

```python
import functools
import math

import jax
import jax.numpy as jnp
from jax import lax
from jax.experimental import pallas as pl
from jax.experimental.pallas import tpu as pltpu

F32 = jnp.float32
BF16 = jnp.bfloat16

D_MODEL = 2048
HEAD_DIM = 128
N_HEADS = 16
N_HEADS_A = 8
N_HEADS_B = 8
WIDTH_A = N_HEADS_A * HEAD_DIM
WIDTH_B = N_HEADS_B * HEAD_DIM
DIL_CONFIGS = ((128, 1), (512, 4), (2048, 16))
DIL_BLOCK = 128
NSA_KV_GROUPS = 2
NSA_HPG = N_HEADS_B // NSA_KV_GROUPS
KV_WIDTH_B = NSA_KV_GROUPS * HEAD_DIM
CMP_BLOCK = 32
CMP_STRIDE = 16
CMP_HIDDEN = 2 * HEAD_DIM
SEL_BLOCK = 64
SEL_TOP_N = 16
SEL_FORCED_LOCAL = 2
FORCE_BONUS = 1.0e3
WIN_SIZE = 512
N_GATES = 3
FFN_HIDDEN = 5632
RMS_EPS = 1e-6
NEG_INF = -1e30
LOG2_E = math.log2(math.e)

COLS_A = 3 * WIDTH_A
COLS_B = WIDTH_B + 6 * KV_WIDTH_B
COLS_MAIN = COLS_A + COLS_B
LANE = 128
GATE_PAD = NSA_KV_GROUPS * LANE

VMEM_LIMIT = 56 * 1024 * 1024


def _cparams(sem):
    return pltpu.CompilerParams(dimension_semantics=sem, vmem_limit_bytes=VMEM_LIMIT)


def _nt_dot(a, b):
    return lax.dot_general(a, b, (((1,), (1,)), ((), ())), preferred_element_type=F32)


IN_TM = 1024
IN_TN = 512
N_TILES_A = COLS_A // IN_TN
N_TILES_B = COLS_B // IN_TN
Q_B_TILE0 = N_TILES_A


def _inproj_kernel(x_ref, g_ref, w_ref, wg_ref, o_ref, og_ref, xn_ref, *, scale):
    j = pl.program_id(1)

    @pl.when(j == 0)
    def _():
        x = x_ref[...]
        y = x * lax.rsqrt(jnp.mean(x * x, axis=-1, keepdims=True) + RMS_EPS)
        xn = (y * g_ref[...]).astype(BF16)
        xn_ref[...] = xn
        og_ref[...] = jnp.dot(xn, wg_ref[...], preferred_element_type=F32)

    acc = jnp.dot(xn_ref[...], w_ref[...], preferred_element_type=F32)
    col_scale = jnp.where(j < 2, scale,
                          jnp.where((j >= Q_B_TILE0) & (j < Q_B_TILE0 + 2), scale * LOG2_E, 1.0))
    o_ref[...] = (acc * col_scale.astype(F32)).astype(BF16)


def _in_projection(x2, g, w_main, w_gate):
    T = x2.shape[0]
    grid = (T // IN_TM, N_TILES_A + N_TILES_B)
    return pl.pallas_call(
        functools.partial(_inproj_kernel, scale=1.0 / math.sqrt(HEAD_DIM)),
        grid=grid,
        in_specs=[
            pl.BlockSpec((IN_TM, D_MODEL), lambda i, j: (i, 0)),
            pl.BlockSpec((1, D_MODEL), lambda i, j: (0, 0)),
            pl.BlockSpec((D_MODEL, IN_TN), lambda i, j: (0, j)),
            pl.BlockSpec((D_MODEL, GATE_PAD), lambda i, j: (0, 0)),
        ],
        out_specs=[
            pl.BlockSpec((IN_TM, IN_TN), lambda i, j: (i, j)),
            pl.BlockSpec((IN_TM, GATE_PAD), lambda i, j: (i, 0)),
        ],
        out_shape=[
            jax.ShapeDtypeStruct((T, COLS_MAIN), BF16),
            jax.ShapeDtypeStruct((T, GATE_PAD), F32),
        ],
        scratch_shapes=[pltpu.VMEM((IN_TM, D_MODEL), BF16)],
        compiler_params=_cparams(("arbitrary", "arbitrary")),
        name="in_projection",
    )(x2, g, w_main, w_gate)


DIL_HEAD_GROUP = 4
DIL_COLS = DIL_HEAD_GROUP * HEAD_DIM
DIL_LSE_W = (N_HEADS_A // DIL_HEAD_GROUP) * LANE
DIL_PERM = 256
D1_BLOCKS = 2


def _dilated_heads(q, k_prev, k_cur, v_prev, v_cur, slopes, has_prev, dilation, span):
    key = lax.broadcasted_iota(jnp.int32, (2 * DIL_BLOCK, DIL_BLOCK), 0)
    qry = lax.broadcasted_iota(jnp.int32, (2 * DIL_BLOCK, DIL_BLOCK), 1)
    rel = qry + DIL_BLOCK - key
    in_prev = jnp.where(rel <= span, jnp.where(has_prev, 0.0, NEG_INF), NEG_INF)
    mask_bias = jnp.where(key < DIL_BLOCK, in_prev, jnp.where(rel >= 0, 0.0, NEG_INF))
    dist = (rel * dilation).astype(F32)
    heads = [slice(h * HEAD_DIM, (h + 1) * HEAD_DIM) for h in range(len(slopes))]
    scores = [_nt_dot(jnp.concatenate([k_prev[:, hs], k_cur[:, hs]], axis=0), q[:, hs]) for hs in heads]
    scores = [s - slope * dist + mask_bias for slope, s in zip(slopes, scores)]
    maxes = [jnp.max(s, axis=0, keepdims=True) for s in scores]
    probs = [jnp.exp(s - m) for s, m in zip(scores, maxes)]
    sums = [jnp.sum(p, axis=0, keepdims=True) for p in probs]
    outs = [lax.dot_general((p * (1.0 / l)).astype(BF16),
                            jnp.concatenate([v_prev[:, hs], v_cur[:, hs]], axis=0),
                            (((0,), (0,)), ((), ())), preferred_element_type=F32)
            for hs, p, l in zip(heads, probs, sums)]
    return (jnp.concatenate(outs, axis=1).astype(BF16),
            [m + jnp.log(l) for m, l in zip(maxes, sums)])


def _lse_tile(rows):
    pad = jnp.zeros((LANE - len(rows), DIL_BLOCK), F32)
    return jnp.concatenate(list(rows) + [pad], axis=0).T


def _dilated_kernel(slopes_ref, q_ref, kc_ref, kp_ref, vc_ref, vp_ref, o_ref, lse_ref, *,
                    dilation, span):
    slopes = [slopes_ref[h] for h in range(N_HEADS_A)]
    for i in range(D1_BLOCKS):
        rows = slice(i * DIL_BLOCK, (i + 1) * DIL_BLOCK)
        before = slice((i - 1) * DIL_BLOCK, i * DIL_BLOCK)
        k_prev, v_prev = (kp_ref[0], vp_ref[0]) if i == 0 else (kc_ref[0, before, :], vc_ref[0, before, :])
        has_prev = (pl.program_id(2) > 0) if i == 0 else True
        o, lse = _dilated_heads(q_ref[0, rows, :], k_prev, kc_ref[0, rows, :], v_prev, vc_ref[0, rows, :],
                                slopes, has_prev, dilation, span)
        o_ref[0, rows, :] = o
        for grp in range(N_HEADS_A // DIL_HEAD_GROUP):
            lse_ref[0, rows, grp * LANE:(grp + 1) * LANE] = _lse_tile(
                lse[grp * DIL_HEAD_GROUP:(grp + 1) * DIL_HEAD_GROUP])


def _split_bf16(x, pieces):
    out, rest = [], x
    for _ in range(pieces):
        part = rest.astype(BF16)
        out.append(part)
        rest = rest - part.astype(F32)
    return out


def _strided_kernel(slopes_ref, perm_ref, perm_t_ref, q_ref, k_ref, v_ref, o_ref, lse_ref,
                    qp_ref, kp_ref, vp_ref, op_ref, lp_ref, *, dilation, span):
    grp = pl.program_id(1)
    n = pl.program_id(2)
    n_sub = q_ref.shape[1] // DIL_PERM
    width = DIL_PERM // dilation
    cur = n & 1
    prev = 1 - cur
    perm = perm_ref[...]

    def deinterleave(x):
        y = jnp.dot(perm, x, preferred_element_type=F32).astype(BF16)
        return y.reshape(dilation, width, x.shape[1])

    for sub in range(n_sub):
        rows = slice(sub * DIL_PERM, (sub + 1) * DIL_PERM)
        qp_ref[:, sub] = deinterleave(q_ref[0, rows, :])
        kp_ref[cur, :, sub] = deinterleave(k_ref[0, rows, :])
        vp_ref[cur, :, sub] = deinterleave(v_ref[0, rows, :])

    @pl.when(n == 0)
    def _():
        kp_ref[prev] = jnp.zeros(kp_ref.shape[1:], BF16)
        vp_ref[prev] = jnp.zeros(vp_ref.shape[1:], BF16)

    slopes = [slopes_ref[grp * DIL_HEAD_GROUP + i] for i in range(DIL_HEAD_GROUP)]
    blk_rows = lambda a: a.reshape(DIL_BLOCK, a.shape[-1])

    def residues(i, carry):
        for r in (2 * i, 2 * i + 1):
            o, lse = _dilated_heads(blk_rows(qp_ref[r]), blk_rows(kp_ref[prev, r]), blk_rows(kp_ref[cur, r]),
                                    blk_rows(vp_ref[prev, r]), blk_rows(vp_ref[cur, r]), slopes, n > 0,
                                    dilation, span)
            op_ref[r] = o.reshape(n_sub, width, DIL_COLS)
            lp_ref[r] = _lse_tile(lse).reshape(n_sub, width, LANE)
        return carry

    lax.fori_loop(0, dilation // 2, residues, 0)

    perm_t = perm_t_ref[...]
    for sub in range(n_sub):
        rows = slice(sub * DIL_PERM, (sub + 1) * DIL_PERM)
        o_sub = op_ref[:, sub].reshape(DIL_PERM, DIL_COLS)
        o_ref[0, rows, :] = jnp.dot(perm_t, o_sub, preferred_element_type=F32).astype(BF16)
        l_sub = lp_ref[:, sub].reshape(DIL_PERM, LANE)
        lse_ref[0, rows, :] = sum(jnp.dot(perm_t, part, preferred_element_type=F32)
                                  for part in _split_bf16(l_sub, 3))


def _contiguous_attention(qkv_a, slopes_a, B, S, window):
    rows = D1_BLOCKS * DIL_BLOCK
    assert S % rows == 0
    a3 = qkv_a.reshape(B, S, COLS_MAIN)
    blk = (1, rows, WIDTH_A)
    one = (1, DIL_BLOCK, WIDTH_A)
    prev = lambda n: jnp.maximum(n * D1_BLOCKS - 1, 0)
    o, lse = pl.pallas_call(
        functools.partial(_dilated_kernel, dilation=1, span=window),
        grid=(B, 1, S // rows),
        in_specs=[
            pl.BlockSpec(memory_space=pltpu.SMEM),
            pl.BlockSpec(blk, lambda b, r, n: (b, n, 0)),
            pl.BlockSpec(blk, lambda b, r, n: (b, n, 1)),
            pl.BlockSpec(one, lambda b, r, n: (b, prev(n), 1)),
            pl.BlockSpec(blk, lambda b, r, n: (b, n, 2)),
            pl.BlockSpec(one, lambda b, r, n: (b, prev(n), 2)),
        ],
        out_specs=[
            pl.BlockSpec(blk, lambda b, r, n: (b, n, 0)),
            pl.BlockSpec((1, rows, DIL_LSE_W), lambda b, r, n: (b, n, 0)),
        ],
        out_shape=[
            jax.ShapeDtypeStruct((B, S, WIDTH_A), BF16),
            jax.ShapeDtypeStruct((B, S, DIL_LSE_W), F32),
        ],
        compiler_params=_cparams(("arbitrary", "arbitrary", "arbitrary")),
        name="dilated_attention_d1",
    )(slopes_a, a3, a3, a3, a3, a3)
    return o.reshape(B * S, WIDTH_A), lse.reshape(B * S, DIL_LSE_W)


def _permutation(dilation):
    width = DIL_PERM // dilation
    out_row = jnp.arange(DIL_PERM)
    src = (out_row % width) * dilation + out_row // width
    return (src[:, None] == jnp.arange(DIL_PERM)[None, :]).astype(BF16)


def _strided_attention(qkv_a, slopes_a, B, S, window, dilation):
    rows = DIL_BLOCK * dilation
    assert S % rows == 0 and rows % DIL_PERM == 0 and DIL_PERM % dilation == 0
    assert (DIL_PERM // dilation) % 16 == 0
    assert dilation % 2 == 0
    n_sub, width = rows // DIL_PERM, DIL_PERM // dilation
    n_grp = N_HEADS_A // DIL_HEAD_GROUP
    per = WIDTH_A // DIL_COLS
    a3 = qkv_a.reshape(B, S, COLS_MAIN)
    perm = _permutation(dilation)
    blk = (1, rows, DIL_COLS)
    whole = lambda b, g, n: (0, 0)
    packed = lambda dt: pltpu.VMEM((dilation, n_sub, width, DIL_COLS), dt)
    o, lse = pl.pallas_call(
        functools.partial(_strided_kernel, dilation=dilation, span=window // dilation),
        grid=(B, n_grp, S // rows),
        in_specs=[
            pl.BlockSpec(memory_space=pltpu.SMEM),
            pl.BlockSpec((DIL_PERM, DIL_PERM), whole),
            pl.BlockSpec((DIL_PERM, DIL_PERM), whole),
            pl.BlockSpec(blk, lambda b, g, n: (b, n, g)),
            pl.BlockSpec(blk, lambda b, g, n: (b, n, per + g)),
            pl.BlockSpec(blk, lambda b, g, n: (b, n, 2 * per + g)),
        ],
        out_specs=[
            pl.BlockSpec(blk, lambda b, g, n: (b, n, g)),
            pl.BlockSpec((1, rows, LANE), lambda b, g, n: (b, n, g)),
        ],
        out_shape=[
            jax.ShapeDtypeStruct((B, S, WIDTH_A), BF16),
            jax.ShapeDtypeStruct((B, S, DIL_LSE_W), F32),
        ],
        scratch_shapes=[
            packed(BF16),
            pltpu.VMEM((2, dilation, n_sub, width, DIL_COLS), BF16),
            pltpu.VMEM((2, dilation, n_sub, width, DIL_COLS), BF16),
            packed(BF16),
            pltpu.VMEM((dilation, n_sub, width, LANE), F32),
        ],
        compiler_params=_cparams(("arbitrary", "arbitrary", "arbitrary")),
        name=f"dilated_attention_d{dilation}",
    )(slopes_a, perm, perm.T, a3, a3, a3)
    return o.reshape(B * S, WIDTH_A), lse.reshape(B * S, DIL_LSE_W)


CHUNK_FLAT = CMP_STRIDE * HEAD_DIM


def _compress_kernel(x_ref, pe_ref, w1_ref, w2_ref, o_ref):
    x = x_ref[0, 0].astype(F32)
    first = jnp.dot((x + pe_ref[0:1, :]).astype(BF16), w1_ref[0:CHUNK_FLAT, :],
                    preferred_element_type=F32)
    second = jnp.dot((x + pe_ref[1:2, :]).astype(BF16), w1_ref[CHUNK_FLAT:2 * CHUNK_FLAT, :],
                     preferred_element_type=F32)
    nc = x.shape[0]
    hidden = first + pltpu.roll(second, shift=nc - 1, axis=0)
    act = hidden * jax.nn.sigmoid(hidden)
    o_ref[0, 0] = jnp.dot(act.astype(BF16), w2_ref[...], preferred_element_type=F32).astype(BF16)


def _compress(chunks, pe2, w1, w2):
    B, G, nc, _ = chunks.shape
    return pl.pallas_call(
        _compress_kernel,
        grid=(B, G),
        in_specs=[
            pl.BlockSpec((1, 1, nc, CHUNK_FLAT), lambda b, g: (b, g, 0, 0)),
            pl.BlockSpec((2, CHUNK_FLAT), lambda b, g: (0, 0)),
            pl.BlockSpec((2 * CHUNK_FLAT, CMP_HIDDEN), lambda b, g: (0, 0)),
            pl.BlockSpec((CMP_HIDDEN, HEAD_DIM), lambda b, g: (0, 0)),
        ],
        out_specs=pl.BlockSpec((1, 1, nc, HEAD_DIM), lambda b, g: (b, g, 0, 0)),
        out_shape=jax.ShapeDtypeStruct((B, G, nc, HEAD_DIM), BF16),
        compiler_params=_cparams(("arbitrary", "arbitrary")),
        name="nsa_compress",
    )(chunks, pe2, w1, w2)


NSA_TQ = 256
NSA_TK = 512
WIN_SPAN = WIN_SIZE + NSA_TQ
SLOPE_PIECES = 3


def _stack_heads(a):
    return jnp.concatenate([a[:, h * HEAD_DIM:(h + 1) * HEAD_DIM] for h in range(NSA_HPG)], axis=0)


def _nsa_kernel(slopes_ref, q_ref, kc_ref, vct_ref, ks_ref, vst_ref, kw_ref, vwt_ref, kf_ref, gate_ref,
                ovl_ref, o_ref, *, n_sel):
    g = pl.program_id(1)
    qi = pl.program_id(2)
    t0 = qi * NSA_TQ
    cols = NSA_HPG * NSA_TQ

    q = _stack_heads(q_ref[0])
    slope_row = jnp.concatenate(
        [jnp.full((1, NSA_TQ), slopes_ref[SLOPE_PIECES, g * NSA_HPG + h], F32) for h in range(NSA_HPG)],
        axis=1)
    t_row = t0 + (lax.broadcasted_iota(jnp.int32, (1, cols), 1) & (NSA_TQ - 1))

    n_cmp_pad = kc_ref.shape[2]
    cend = (lax.broadcasted_iota(jnp.int32, (n_cmp_pad, 1), 0) * CMP_STRIDE + (CMP_BLOCK - 1))
    rel_c = t_row - cend
    ok_c = rel_c >= 0
    s = _nt_dot(kc_ref[0, 0], q)
    s = s - slope_row * rel_c.astype(F32)
    s = jnp.where(ok_c, s, NEG_INF)
    m = jnp.max(s, axis=0, keepdims=True)
    e = jnp.where(ok_c, jnp.exp2(s - m), 0.0)
    p = e / jnp.maximum(jnp.sum(e, axis=0, keepdims=True), 1e-30)
    o_cmp = jnp.dot(vct_ref[0, 0], p.astype(BF16), preferred_element_type=F32)

    lane = lax.broadcasted_iota(jnp.int32, (1, LANE), 1)
    slope_feat = []
    for h in range(NSA_HPG):
        row = jnp.zeros((1, LANE), F32)
        for part in range(2):
            for piece in range(SLOPE_PIECES):
                row = jnp.where(lane == n_sel + part * SLOPE_PIECES + piece,
                                slopes_ref[piece, g * NSA_HPG + h], row)
        slope_feat.append(jnp.broadcast_to(row, (NSA_TQ, LANE)))
    slope_feat = jnp.concatenate(slope_feat, axis=0)
    q_win = jnp.concatenate([q, slope_feat.astype(BF16)], axis=1)

    def keys_aug(k_ref, start, size):
        return jnp.concatenate([k_ref[0, pl.ds(start, size), :], kf_ref[pl.ds(start, size), :]], axis=1)

    n_pieces = WIN_SIZE // NSA_TQ + 1
    win_scores, win_starts = [], []
    for i in range(n_pieces):
        first = t0 - WIN_SIZE + i * NSA_TQ
        start = pl.multiple_of(jnp.maximum(first, 0), NSA_TQ)
        s_i = _nt_dot(keys_aug(kw_ref, start, NSA_TQ), q_win)
        rel_i = t_row - (first + lax.broadcasted_iota(jnp.int32, (NSA_TQ, 1), 0))
        if i == n_pieces - 1:
            s_i = jnp.where(rel_i >= 0, s_i, NEG_INF)
        elif i == 0:
            s_i = jnp.where(rel_i < jnp.where(first >= 0, WIN_SIZE, -WIN_SPAN), s_i, NEG_INF)
        else:
            s_i = s_i + jnp.where(first >= 0, 0.0, NEG_INF)
        win_scores.append(s_i)
        win_starts.append(start)
    m_w = functools.reduce(jnp.maximum, [jnp.max(s_i, axis=0, keepdims=True) for s_i in win_scores])
    win_probs = [jnp.exp2(s_i - m_w) for s_i in win_scores]
    l_w = sum(jnp.sum(p_i, axis=0, keepdims=True) for p_i in win_probs)
    o_win = sum(jnp.dot(vwt_ref[0, 0, :, pl.ds(start, NSA_TQ)], p_i.astype(BF16), preferred_element_type=F32)
                for start, p_i in zip(win_starts, win_probs)) / l_w

    p_grp = p[:, 0:NSA_TQ]
    for h in range(1, NSA_HPG):
        p_grp = p_grp + p[:, h * NSA_TQ:(h + 1) * NSA_TQ]
    p_hi = p_grp.astype(BF16)
    p_lo = (p_grp - p_hi.astype(F32)).astype(BF16)
    ovl = ovl_ref[...]
    imp_t = (jnp.dot(ovl, p_hi, preferred_element_type=F32)
             + jnp.dot(ovl, p_lo, preferred_element_type=F32))
    blk = lax.broadcasted_iota(jnp.int32, (n_sel, NSA_TQ), 0)
    cur = (t0 + lax.broadcasted_iota(jnp.int32, (n_sel, NSA_TQ), 1)) // SEL_BLOCK
    valid = blk <= cur
    forced = (blk == 0) | (valid & (blk > cur - SEL_FORCED_LOCAL))
    score = jnp.where(valid, imp_t + jnp.where(forced, FORCE_BONUS, 0.0), -1.0)
    rank = jnp.zeros((n_sel, NSA_TQ), jnp.int32)
    for mth in range(n_sel):
        other = score[mth:mth + 1, :]
        tie_first = jnp.where(blk > mth, 1, 0)
        rank = rank + jnp.where(other > score, 1, jnp.where(other == score, tie_first, 0))
    member_t = jnp.where(valid, jnp.where(rank < SEL_TOP_N, 1.0, 0.0), 0.0).astype(BF16)
    qrow =lax.broadcasted_iota(jnp.int32, (cols, NSA_TQ), 0) & (NSA_TQ - 1)
    eye = jnp.where(qrow == lax.broadcasted_iota(jnp.int32, (cols, NSA_TQ), 1), 1.0, 0.0).astype(BF16)
    member_pad = jnp.concatenate([member_t, jnp.ones((LANE - n_sel, NSA_TQ), BF16)], axis=0)
    picked = _nt_dot(eye, member_pad)
    q_sel = jnp.concatenate([q, (jnp.where(picked > 0.5, 0.0, NEG_INF) + slope_feat).astype(BF16)], axis=1)

    def sel_step(j, carry, causal):
        m_i, l_i, acc = carry
        start = pl.multiple_of(j * NSA_TK, NSA_TK)
        s_ = _nt_dot(keys_aug(ks_ref, start, NSA_TK), q_sel)
        if causal:
            pos = start + lax.broadcasted_iota(jnp.int32, (NSA_TK, 1), 0)
            s_ = jnp.where(pos <= t_row, s_, NEG_INF)
        m_new = jnp.maximum(m_i, jnp.max(s_, axis=0, keepdims=True))
        alpha = jnp.exp2(m_i - m_new)
        p_ = jnp.exp2(s_ - m_new)
        l_new = alpha * l_i + jnp.sum(p_, axis=0, keepdims=True)
        acc_new = alpha * acc + jnp.dot(vst_ref[0, 0, :, pl.ds(start, NSA_TK)], p_.astype(BF16),
                                        preferred_element_type=F32)
        return m_new, l_new, acc_new

    init = (jnp.full((1, cols), NEG_INF, F32), jnp.zeros((1, cols), F32),
            jnp.zeros((HEAD_DIM, cols), F32))
    last = (t0 + NSA_TQ - 1) // NSA_TK
    carry = lax.fori_loop(0, last, functools.partial(sel_step, causal=False), init)
    _, l_s, acc_s = sel_step(last, carry, causal=True)
    o_sel = acc_s / l_s

    gates_t = jax.nn.sigmoid(gate_ref[0]).T
    for h in range(NSA_HPG):
        cs = slice(h * NSA_TQ, (h + 1) * NSA_TQ)
        gsel = [gates_t[h * N_GATES + k:h * N_GATES + k + 1, :] for k in range(N_GATES)]
        mix_t = gsel[0] * o_cmp[:, cs] + gsel[1] * o_sel[:, cs] + gsel[2] * o_win[:, cs]
        o_ref[0, :, h * HEAD_DIM:(h + 1) * HEAD_DIM] = mix_t.T


def _key_features(S):
    pos = jnp.arange(S)[:, None]
    lane = jnp.arange(LANE)[None, :]
    n_sel = S // SEL_BLOCK
    feat = jnp.where(lane == pos // SEL_BLOCK, 1, 0)
    feat = jnp.where((lane >= n_sel) & (lane < n_sel + SLOPE_PIECES), pos % SEL_BLOCK, feat)
    feat = jnp.where((lane >= n_sel + SLOPE_PIECES) & (lane < n_sel + 2 * SLOPE_PIECES),
                     (pos // SEL_BLOCK) * SEL_BLOCK, feat)
    return feat.astype(BF16)


def _slope_table(slopes):
    rows, rest = [], slopes
    for _ in range(SLOPE_PIECES):
        piece = rest.astype(BF16).astype(F32)
        rows.append(piece)
        rest = rest - piece
    return jnp.stack(rows + [slopes])


def _nsa_attention(qkv_b, kc, vc_t, vs_t, vw_t, gates, slopes_b, overlap_t, B, S):
    assert S % NSA_TK == 0 and S >= WIN_SPAN
    assert S // SEL_BLOCK <= SEL_BLOCK and S // SEL_BLOCK + 2 * SLOPE_PIECES <= LANE
    b3 = qkv_b.reshape(B, S, COLS_MAIN)
    g3 = gates.reshape(B, S, GATE_PAD)
    n_cmp_pad = kc.shape[2]
    n_sel = S // SEL_BLOCK
    q_w = NSA_HPG * HEAD_DIM
    q0 = COLS_A // q_w
    kv0 = (COLS_A + WIDTH_B) // HEAD_DIM
    per = KV_WIDTH_B // HEAD_DIM
    seq_blk = (1, S, HEAD_DIM)
    seq_t_blk = (1, 1, HEAD_DIM, S)
    grp = lambda b, g, i: (b, g, 0, 0)
    out = pl.pallas_call(
        functools.partial(_nsa_kernel, n_sel=n_sel),
        grid=(B, NSA_KV_GROUPS, S // NSA_TQ),
        in_specs=[
            pl.BlockSpec(memory_space=pltpu.SMEM),
            pl.BlockSpec((1, NSA_TQ, q_w), lambda b, g, i: (b, i, q0 + g)),
            pl.BlockSpec((1, 1, n_cmp_pad, HEAD_DIM), grp),
            pl.BlockSpec((1, 1, HEAD_DIM, n_cmp_pad), grp),
            pl.BlockSpec(seq_blk, lambda b, g, i: (b, 0, kv0 + 2 * per + g)),
            pl.BlockSpec(seq_t_blk, grp),
            pl.BlockSpec(seq_blk, lambda b, g, i: (b, 0, kv0 + 4 * per + g)),
            pl.BlockSpec(seq_t_blk, grp),
            pl.BlockSpec((S, LANE), lambda b, g, i: (0, 0)),
            pl.BlockSpec((1, NSA_TQ, LANE), lambda b, g, i: (b, i, g)),
            pl.BlockSpec((n_sel, n_cmp_pad), lambda b, g, i: (0, 0)),
        ],
        out_specs=pl.BlockSpec((1, NSA_TQ, q_w), lambda b, g, i: (b, i, g)),
        out_shape=jax.ShapeDtypeStruct((B, S, WIDTH_B), F32),
        compiler_params=_cparams(("arbitrary", "arbitrary", "arbitrary")),
        name="nsa_attention",
    )(_slope_table(slopes_b * LOG2_E), b3, kc, vc_t, b3, vs_t, b3, vw_t, _key_features(S), g3, overlap_t)
    return out.reshape(B * S, WIDTH_B)


OUT_TM = 512
OUT_SUB = 256


def _rms(x, g):
    return x * lax.rsqrt(jnp.mean(x * x, axis=-1, keepdims=True) + RMS_EPS) * g


def _outproj_kernel(o1_ref, o2_ref, o3_ref, l1_ref, l2_ref, l3_ref, ob_ref, x_ref, ga_ref, gb_ref,
                    w_ref, g2_ref, h_ref, hn_ref):
    for sub in range(OUT_TM // OUT_SUB):
        rows = slice(sub * OUT_SUB, (sub + 1) * OUT_SUB)
        l1, l2, l3 = l1_ref[rows, :], l2_ref[rows, :], l3_ref[rows, :]
        mx = jnp.maximum(jnp.maximum(l1, l2), l3)
        e1, e2, e3 = jnp.exp(l1 - mx), jnp.exp(l2 - mx), jnp.exp(l3 - mx)
        den = e1 + e2 + e3
        w1, w2, w3 = e1 / den, e2 / den, e3 / den
        parts = []
        for h in range(N_HEADS_A):
            hs = slice(h * HEAD_DIM, (h + 1) * HEAD_DIM)
            c = (h // DIL_HEAD_GROUP) * LANE + h % DIL_HEAD_GROUP
            parts.append(w1[:, c:c + 1] * o1_ref[rows, hs] + w2[:, c:c + 1] * o2_ref[rows, hs]
                         + w3[:, c:c + 1] * o3_ref[rows, hs])
        o_a = jnp.concatenate(parts, axis=-1)
        mixed = jnp.concatenate([_rms(o_a, ga_ref[...]), _rms(ob_ref[rows, :], gb_ref[...])], axis=-1)
        h = x_ref[rows, :] + jnp.dot(mixed.astype(BF16), w_ref[...], preferred_element_type=F32)
        h_ref[rows, :] = h
        hn_ref[rows, :] = _rms(h, g2_ref[...]).astype(BF16)


def _out_projection(o_as, lses, o_b, x2, ga, gb, w_out, g2):
    T = x2.shape[0]
    row = lambda w: pl.BlockSpec((OUT_TM, w), lambda i: (i, 0))
    full = lambda r, c: pl.BlockSpec((r, c), lambda i: (0, 0))
    return pl.pallas_call(
        _outproj_kernel,
        grid=(T // OUT_TM,),
        in_specs=[row(WIDTH_A)] * 3 + [row(DIL_LSE_W)] * 3 + [row(WIDTH_B), row(D_MODEL),
                  full(1, WIDTH_A), full(1, WIDTH_B), full(WIDTH_A + WIDTH_B, D_MODEL), full(1, D_MODEL)],
        out_specs=[row(D_MODEL), row(D_MODEL)],
        out_shape=[jax.ShapeDtypeStruct((T, D_MODEL), F32), jax.ShapeDtypeStruct((T, D_MODEL), BF16)],
        compiler_params=_cparams(("arbitrary",)),
        name="out_projection",
    )(*o_as, *lses, o_b, x2, ga, gb, w_out, g2)


FFN_TM = 512
FFN_TH = 512


def _ffn_kernel(hn_ref, h_ref, wg_ref, wu_ref, wd_ref, gf_ref, o_ref, acc_ref):
    j = pl.program_id(1)

    @pl.when(j == 0)
    def _():
        acc_ref[...] = jnp.zeros(acc_ref.shape, F32)

    hn = hn_ref[...]
    gate = jnp.dot(hn, wg_ref[...], preferred_element_type=F32)
    up = jnp.dot(hn, wu_ref[...], preferred_element_type=F32)
    act = (gate * jax.nn.sigmoid(gate) * up).astype(BF16)
    acc_ref[...] += jnp.dot(act, wd_ref[...], preferred_element_type=F32)

    @pl.when(j == pl.num_programs(1) - 1)
    def _():
        o_ref[...] = _rms(h_ref[...] + acc_ref[...], gf_ref[...])


def _ffn(hn2, h, w_gate, w_up, w_down, gf):
    T = h.shape[0]
    return pl.pallas_call(
        _ffn_kernel,
        grid=(T // FFN_TM, FFN_HIDDEN // FFN_TH),
        in_specs=[
            pl.BlockSpec((FFN_TM, D_MODEL), lambda i, j: (i, 0)),
            pl.BlockSpec((FFN_TM, D_MODEL), lambda i, j: (i, 0)),
            pl.BlockSpec((D_MODEL, FFN_TH), lambda i, j: (0, j)),
            pl.BlockSpec((D_MODEL, FFN_TH), lambda i, j: (0, j)),
            pl.BlockSpec((FFN_TH, D_MODEL), lambda i, j: (j, 0)),
            pl.BlockSpec((1, D_MODEL), lambda i, j: (0, 0)),
        ],
        out_specs=pl.BlockSpec((FFN_TM, D_MODEL), lambda i, j: (i, 0)),
        out_shape=jax.ShapeDtypeStruct((T, D_MODEL), F32),
        scratch_shapes=[pltpu.VMEM((FFN_TM, D_MODEL), F32)],
        compiler_params=_cparams(("arbitrary", "arbitrary")),
        name="swiglu_ffn",
    )(hn2, h, w_gate, w_up, w_down, gf)


def _chunk_view(qkv_b3, col0):
    B, S, _ = qkv_b3.shape
    a = qkv_b3[:, :, col0:col0 + KV_WIDTH_B]
    a = a.reshape(B, S // CMP_STRIDE, CMP_STRIDE, NSA_KV_GROUPS, HEAD_DIM)
    return a.transpose(0, 3, 1, 2, 4).reshape(B, NSA_KV_GROUPS, S // CMP_STRIDE, CHUNK_FLAT)


def _overlap_t(n_sel, n_cmp_pad):
    cstart = jnp.arange(n_cmp_pad)[None, :] * CMP_STRIDE
    sstart = jnp.arange(n_sel)[:, None] * SEL_BLOCK
    return ((cstart < sstart + SEL_BLOCK) & (cstart + CMP_BLOCK > sstart)).astype(BF16)


def kernel(x, norm1_g, w_in, cmp_pe_k, cmp_w1_k, cmp_w2_k, cmp_pe_v, cmp_w1_v, cmp_w2_v, grp_norm_a,
           grp_norm_b, w_out, norm2_g, w_gate, w_up, w_down, final_g):
    B, S, D = x.shape
    assert D == D_MODEL and S % (DIL_BLOCK * max(d for _, d in DIL_CONFIGS)) == 0
    assert (B * S) % IN_TM == 0 and w_in.shape[0] == 1, "single-layer block only"
    T = B * S
    slopes = jnp.exp2(-8.0 * jnp.arange(1, N_HEADS + 1, dtype=F32) / N_HEADS)
    slopes_a, slopes_b = slopes[0::2], slopes[1::2]
    n_main = COLS_A + COLS_B
    per_group = NSA_HPG * N_GATES

    x2 = x.reshape(T, D)
    w_main = w_in[0][:, :n_main].astype(BF16)
    w_g = w_in[0][:, n_main:].reshape(D, NSA_KV_GROUPS, per_group)
    w_g = jnp.pad(w_g, ((0, 0), (0, 0), (0, LANE - per_group))).reshape(D, GATE_PAD).astype(BF16)
    proj, gates = _in_projection(x2, norm1_g[0][None, :], w_main, w_g)

    dil = [_contiguous_attention(proj, slopes_a, B, S, w) if d == 1
           else _strided_attention(proj, slopes_a, B, S, w, d) for w, d in DIL_CONFIGS]

    b3 = proj.reshape(B, S, COLS_MAIN)
    kv_col = COLS_A + WIDTH_B
    pe2 = lambda pe: pe.reshape(2, CHUNK_FLAT)
    kc = _compress(_chunk_view(b3, kv_col), pe2(cmp_pe_k[0]), cmp_w1_k[0].astype(BF16),
                   cmp_w2_k[0].astype(BF16))
    vc = _compress(_chunk_view(b3, kv_col + KV_WIDTH_B), pe2(cmp_pe_v[0]), cmp_w1_v[0].astype(BF16),
                   cmp_w2_v[0].astype(BF16))
    col_vs = kv_col + 3 * KV_WIDTH_B
    col_vw = kv_col + 5 * KV_WIDTH_B
    seq_t = lambda c0: b3[:, :, c0:c0 + KV_WIDTH_B].reshape(B, S, NSA_KV_GROUPS, HEAD_DIM).transpose(0, 2, 3, 1)
    o_b = _nsa_attention(proj, kc, vc.transpose(0, 1, 3, 2), seq_t(col_vs), seq_t(col_vw), gates, slopes_b,
                         _overlap_t(S // SEL_BLOCK, kc.shape[2]), B, S)

    h, hn2 = _out_projection([o for o, _ in dil], [s for _, s in dil], o_b, x2,
                             grp_norm_a[0][None, :], grp_norm_b[0][None, :],
                             w_out[0].astype(BF16), norm2_g[0][None, :])
    out = _ffn(hn2, h, w_gate[0].astype(BF16), w_up[0].astype(BF16), w_down[0].astype(BF16),
               final_g[None, :])
    return out.reshape(B, S, D)
```

```python
import functools
import math

import jax
import jax.numpy as jnp
from jax import lax
from jax.experimental import pallas as pl
from jax.experimental.pallas import tpu as pltpu

F32 = jnp.float32
BF16 = jnp.bfloat16

D_MODEL = 2048
HEAD_DIM = 128
N_HEADS = 16
N_HEADS_A = 8
N_HEADS_B = 8
WIDTH_A = N_HEADS_A * HEAD_DIM
WIDTH_B = N_HEADS_B * HEAD_DIM
DIL_CONFIGS = ((128, 1), (512, 4), (2048, 16))
DIL_BLOCK = 128
NSA_KV_GROUPS = 2
NSA_HPG = N_HEADS_B // NSA_KV_GROUPS
KV_WIDTH_B = NSA_KV_GROUPS * HEAD_DIM
CMP_BLOCK = 32
CMP_STRIDE = 16
CMP_HIDDEN = 2 * HEAD_DIM
SEL_BLOCK = 64
SEL_TOP_N = 16
SEL_FORCED_LOCAL = 2
FORCE_BONUS = 1.0e3
WIN_SIZE = 512
N_GATES = 3
FFN_HIDDEN = 5632
RMS_EPS = 1e-6
NEG_INF = -1e30
LOG2_E = math.log2(math.e)

COLS_A = 3 * WIDTH_A
COLS_B = WIDTH_B + 6 * KV_WIDTH_B
COLS_MAIN = COLS_A + COLS_B
LANE = 128
GATE_PAD = NSA_KV_GROUPS * LANE

VMEM_LIMIT = 56 * 1024 * 1024


def _cparams(sem):
    return pltpu.CompilerParams(dimension_semantics=sem, vmem_limit_bytes=VMEM_LIMIT)


def _nt_dot(a, b):
    return lax.dot_general(a, b, (((1,), (1,)), ((), ())), preferred_element_type=F32)


IN_TM = 1024
IN_TN = 512
N_TILES_A = COLS_A // IN_TN
N_TILES_B = COLS_B // IN_TN
Q_B_TILE0 = N_TILES_A


def _inproj_kernel(x_ref, g_ref, w_ref, wg_ref, o_ref, og_ref, xn_ref, *, scale):
    j = pl.program_id(1)

    @pl.when(j == 0)
    def _():
        x = x_ref[...]
        y = x * lax.rsqrt(jnp.mean(x * x, axis=-1, keepdims=True) + RMS_EPS)
        xn = (y * g_ref[...]).astype(BF16)
        xn_ref[...] = xn
        og_ref[...] = jnp.dot(xn, wg_ref[...], preferred_element_type=F32)

    acc = jnp.dot(xn_ref[...], w_ref[...], preferred_element_type=F32)
    col_scale = jnp.where(j < 2, scale,
                          jnp.where((j >= Q_B_TILE0) & (j < Q_B_TILE0 + 2), scale * LOG2_E, 1.0))
    o_ref[...] = (acc * col_scale.astype(F32)).astype(BF16)


def _in_projection(x2, g, w_main, w_gate):
    T = x2.shape[0]
    grid = (T // IN_TM, N_TILES_A + N_TILES_B)
    return pl.pallas_call(
        functools.partial(_inproj_kernel, scale=1.0 / math.sqrt(HEAD_DIM)),
        grid=grid,
        in_specs=[
            pl.BlockSpec((IN_TM, D_MODEL), lambda i, j: (i, 0)),
            pl.BlockSpec((1, D_MODEL), lambda i, j: (0, 0)),
            pl.BlockSpec((D_MODEL, IN_TN), lambda i, j: (0, j)),
            pl.BlockSpec((D_MODEL, GATE_PAD), lambda i, j: (0, 0)),
        ],
        out_specs=[
            pl.BlockSpec((IN_TM, IN_TN), lambda i, j: (i, j)),
            pl.BlockSpec((IN_TM, GATE_PAD), lambda i, j: (i, 0)),
        ],
        out_shape=[
            jax.ShapeDtypeStruct((T, COLS_MAIN), BF16),
            jax.ShapeDtypeStruct((T, GATE_PAD), F32),
        ],
        scratch_shapes=[pltpu.VMEM((IN_TM, D_MODEL), BF16)],
        compiler_params=_cparams(("arbitrary", "arbitrary")),
        name="in_projection",
    )(x2, g, w_main, w_gate)


DIL_HEAD_GROUP = 4
DIL_COLS = DIL_HEAD_GROUP * HEAD_DIM
DIL_LSE_W = (N_HEADS_A // DIL_HEAD_GROUP) * LANE
DIL_PERM = 256
D1_BLOCKS = 2


def _dilated_heads(q, k_prev, k_cur, v_prev, v_cur, slopes, has_prev, dilation, span):
    key = lax.broadcasted_iota(jnp.int32, (2 * DIL_BLOCK, DIL_BLOCK), 0)
    qry = lax.broadcasted_iota(jnp.int32, (2 * DIL_BLOCK, DIL_BLOCK), 1)
    rel = qry + DIL_BLOCK - key
    in_prev = jnp.where(rel <= span, jnp.where(has_prev, 0.0, NEG_INF), NEG_INF)
    mask_bias = jnp.where(key < DIL_BLOCK, in_prev, jnp.where(rel >= 0, 0.0, NEG_INF))
    dist = (rel * dilation).astype(F32)
    heads = [slice(h * HEAD_DIM, (h + 1) * HEAD_DIM) for h in range(len(slopes))]
    scores = [_nt_dot(jnp.concatenate([k_prev[:, hs], k_cur[:, hs]], axis=0), q[:, hs]) for hs in heads]
    scores = [s - slope * dist + mask_bias for slope, s in zip(slopes, scores)]
    maxes = [jnp.max(s, axis=0, keepdims=True) for s in scores]
    probs = [jnp.exp(s - m) for s, m in zip(scores, maxes)]
    sums = [jnp.sum(p, axis=0, keepdims=True) for p in probs]
    outs = [lax.dot_general((p * (1.0 / l)).astype(BF16),
                            jnp.concatenate([v_prev[:, hs], v_cur[:, hs]], axis=0),
                            (((0,), (0,)), ((), ())), preferred_element_type=F32)
            for hs, p, l in zip(heads, probs, sums)]
    return (jnp.concatenate(outs, axis=1).astype(BF16),
            [m + jnp.log(l) for m, l in zip(maxes, sums)])


def _lse_tile(rows):
    pad = jnp.zeros((LANE - len(rows), DIL_BLOCK), F32)
    return jnp.concatenate(list(rows) + [pad], axis=0).T


def _dilated_kernel(slopes_ref, q_ref, kc_ref, kp_ref, vc_ref, vp_ref, o_ref, lse_ref, *,
                    dilation, span):
    slopes = [slopes_ref[h] for h in range(N_HEADS_A)]
    for i in range(D1_BLOCKS):
        rows = slice(i * DIL_BLOCK, (i + 1) * DIL_BLOCK)
        before = slice((i - 1) * DIL_BLOCK, i * DIL_BLOCK)
        k_prev, v_prev = (kp_ref[0], vp_ref[0]) if i == 0 else (kc_ref[0, before, :], vc_ref[0, before, :])
        has_prev = (pl.program_id(2) > 0) if i == 0 else True
        o, lse = _dilated_heads(q_ref[0, rows, :], k_prev, kc_ref[0, rows, :], v_prev, vc_ref[0, rows, :],
                                slopes, has_prev, dilation, span)
        o_ref[0, rows, :] = o
        for grp in range(N_HEADS_A // DIL_HEAD_GROUP):
            lse_ref[0, rows, grp * LANE:(grp + 1) * LANE] = _lse_tile(
                lse[grp * DIL_HEAD_GROUP:(grp + 1) * DIL_HEAD_GROUP])


def _split_bf16(x, pieces):
    out, rest = [], x
    for _ in range(pieces):
        part = rest.astype(BF16)
        out.append(part)
        rest = rest - part.astype(F32)
    return out


def _strided_kernel(slopes_ref, perm_ref, perm_t_ref, q_ref, k_ref, v_ref, o_ref, lse_ref,
                    qp_ref, kp_ref, vp_ref, op_ref, lp_ref, *, dilation, span):
    grp = pl.program_id(1)
    n = pl.program_id(2)
    n_sub = q_ref.shape[1] // DIL_PERM
    width = DIL_PERM // dilation
    cur = n & 1
    prev = 1 - cur
    perm = perm_ref[...]

    def deinterleave(x):
        y = jnp.dot(perm, x, preferred_element_type=F32).astype(BF16)
        return y.reshape(dilation, width, x.shape[1])

    for sub in range(n_sub):
        rows = slice(sub * DIL_PERM, (sub + 1) * DIL_PERM)
        qp_ref[:, sub] = deinterleave(q_ref[0, rows, :])
        kp_ref[cur, :, sub] = deinterleave(k_ref[0, rows, :])
        vp_ref[cur, :, sub] = deinterleave(v_ref[0, rows, :])

    @pl.when(n == 0)
    def _():
        kp_ref[prev] = jnp.zeros(kp_ref.shape[1:], BF16)
        vp_ref[prev] = jnp.zeros(vp_ref.shape[1:], BF16)

    slopes = [slopes_ref[grp * DIL_HEAD_GROUP + i] for i in range(DIL_HEAD_GROUP)]
    blk_rows = lambda a: a.reshape(DIL_BLOCK, a.shape[-1])

    def residues(i, carry):
        for r in (2 * i, 2 * i + 1):
            o, lse = _dilated_heads(blk_rows(qp_ref[r]), blk_rows(kp_ref[prev, r]), blk_rows(kp_ref[cur, r]),
                                    blk_rows(vp_ref[prev, r]), blk_rows(vp_ref[cur, r]), slopes, n > 0,
                                    dilation, span)
            op_ref[r] = o.reshape(n_sub, width, DIL_COLS)
            lp_ref[r] = _lse_tile(lse).reshape(n_sub, width, LANE)
        return carry

    lax.fori_loop(0, dilation // 2, residues, 0)

    perm_t = perm_t_ref[...]
    for sub in range(n_sub):
        rows = slice(sub * DIL_PERM, (sub + 1) * DIL_PERM)
        o_sub = op_ref[:, sub].reshape(DIL_PERM, DIL_COLS)
        o_ref[0, rows, :] = jnp.dot(perm_t, o_sub, preferred_element_type=F32).astype(BF16)
        l_sub = lp_ref[:, sub].reshape(DIL_PERM, LANE)
        lse_ref[0, rows, :] = sum(jnp.dot(perm_t, part, preferred_element_type=F32)
                                  for part in _split_bf16(l_sub, 3))


def _contiguous_attention(qkv_a, slopes_a, B, S, window):
    rows = D1_BLOCKS * DIL_BLOCK
    assert S % rows == 0
    a3 = qkv_a.reshape(B, S, COLS_MAIN)
    blk = (1, rows, WIDTH_A)
    one = (1, DIL_BLOCK, WIDTH_A)
    prev = lambda n: jnp.maximum(n * D1_BLOCKS - 1, 0)
    o, lse = pl.pallas_call(
        functools.partial(_dilated_kernel, dilation=1, span=window),
        grid=(B, 1, S // rows),
        in_specs=[
            pl.BlockSpec(memory_space=pltpu.SMEM),
            pl.BlockSpec(blk, lambda b, r, n: (b, n, 0)),
            pl.BlockSpec(blk, lambda b, r, n: (b, n, 1)),
            pl.BlockSpec(one, lambda b, r, n: (b, prev(n), 1)),
            pl.BlockSpec(blk, lambda b, r, n: (b, n, 2)),
            pl.BlockSpec(one, lambda b, r, n: (b, prev(n), 2)),
        ],
        out_specs=[
            pl.BlockSpec(blk, lambda b, r, n: (b, n, 0)),
            pl.BlockSpec((1, rows, DIL_LSE_W), lambda b, r, n: (b, n, 0)),
        ],
        out_shape=[
            jax.ShapeDtypeStruct((B, S, WIDTH_A), BF16),
            jax.ShapeDtypeStruct((B, S, DIL_LSE_W), F32),
        ],
        compiler_params=_cparams(("arbitrary", "arbitrary", "arbitrary")),
        name="dilated_attention_d1",
    )(slopes_a, a3, a3, a3, a3, a3)
    return o.reshape(B * S, WIDTH_A), lse.reshape(B * S, DIL_LSE_W)


def _permutation(dilation):
    width = DIL_PERM // dilation
    out_row = jnp.arange(DIL_PERM)
    src = (out_row % width) * dilation + out_row // width
    return (src[:, None] == jnp.arange(DIL_PERM)[None, :]).astype(BF16)


def _strided_attention(qkv_a, slopes_a, B, S, window, dilation):
    rows = DIL_BLOCK * dilation
    assert S % rows == 0 and rows % DIL_PERM == 0 and DIL_PERM % dilation == 0
    assert (DIL_PERM // dilation) % 16 == 0
    assert dilation % 2 == 0
    n_sub, width = rows // DIL_PERM, DIL_PERM // dilation
    n_grp = N_HEADS_A // DIL_HEAD_GROUP
    per = WIDTH_A // DIL_COLS
    a3 = qkv_a.reshape(B, S, COLS_MAIN)
    perm = _permutation(dilation)
    blk = (1, rows, DIL_COLS)
    whole = lambda b, g, n: (0, 0)
    packed = lambda dt: pltpu.VMEM((dilation, n_sub, width, DIL_COLS), dt)
    o, lse = pl.pallas_call(
        functools.partial(_strided_kernel, dilation=dilation, span=window // dilation),
        grid=(B, n_grp, S // rows),
        in_specs=[
            pl.BlockSpec(memory_space=pltpu.SMEM),
            pl.BlockSpec((DIL_PERM, DIL_PERM), whole),
            pl.BlockSpec((DIL_PERM, DIL_PERM), whole),
            pl.BlockSpec(blk, lambda b, g, n: (b, n, g)),
            pl.BlockSpec(blk, lambda b, g, n: (b, n, per + g)),
            pl.BlockSpec(blk, lambda b, g, n: (b, n, 2 * per + g)),
        ],
        out_specs=[
            pl.BlockSpec(blk, lambda b, g, n: (b, n, g)),
            pl.BlockSpec((1, rows, LANE), lambda b, g, n: (b, n, g)),
        ],
        out_shape=[
            jax.ShapeDtypeStruct((B, S, WIDTH_A), BF16),
            jax.ShapeDtypeStruct((B, S, DIL_LSE_W), F32),
        ],
        scratch_shapes=[
            packed(BF16),
            pltpu.VMEM((2, dilation, n_sub, width, DIL_COLS), BF16),
            pltpu.VMEM((2, dilation, n_sub, width, DIL_COLS), BF16),
            packed(BF16),
            pltpu.VMEM((dilation, n_sub, width, LANE), F32),
        ],
        compiler_params=_cparams(("arbitrary", "arbitrary", "arbitrary")),
        name=f"dilated_attention_d{dilation}",
    )(slopes_a, perm, perm.T, a3, a3, a3)
    return o.reshape(B * S, WIDTH_A), lse.reshape(B * S, DIL_LSE_W)


CHUNK_FLAT = CMP_STRIDE * HEAD_DIM


def _compress_kernel(x_ref, pe_ref, w1_ref, w2_ref, o_ref):
    x = x_ref[0, 0].astype(F32)
    first = jnp.dot((x + pe_ref[0:1, :]).astype(BF16), w1_ref[0:CHUNK_FLAT, :],
                    preferred_element_type=F32)
    second = jnp.dot((x + pe_ref[1:2, :]).astype(BF16), w1_ref[CHUNK_FLAT:2 * CHUNK_FLAT, :],
                     preferred_element_type=F32)
    nc = x.shape[0]
    hidden = first + pltpu.roll(second, shift=nc - 1, axis=0)
    act = hidden * jax.nn.sigmoid(hidden)
    o_ref[0, 0] = jnp.dot(act.astype(BF16), w2_ref[...], preferred_element_type=F32).astype(BF16)


def _compress(chunks, pe2, w1, w2):
    B, G, nc, _ = chunks.shape
    return pl.pallas_call(
        _compress_kernel,
        grid=(B, G),
        in_specs=[
            pl.BlockSpec((1, 1, nc, CHUNK_FLAT), lambda b, g: (b, g, 0, 0)),
            pl.BlockSpec((2, CHUNK_FLAT), lambda b, g: (0, 0)),
            pl.BlockSpec((2 * CHUNK_FLAT, CMP_HIDDEN), lambda b, g: (0, 0)),
            pl.BlockSpec((CMP_HIDDEN, HEAD_DIM), lambda b, g: (0, 0)),
        ],
        out_specs=pl.BlockSpec((1, 1, nc, HEAD_DIM), lambda b, g: (b, g, 0, 0)),
        out_shape=jax.ShapeDtypeStruct((B, G, nc, HEAD_DIM), BF16),
        compiler_params=_cparams(("arbitrary", "arbitrary")),
        name="nsa_compress",
    )(chunks, pe2, w1, w2)


NSA_TQ = 256
NSA_TK = 512
WIN_SPAN = WIN_SIZE + NSA_TQ
SLOPE_PIECES = 3


def _stack_heads(a):
    return jnp.concatenate([a[:, h * HEAD_DIM:(h + 1) * HEAD_DIM] for h in range(NSA_HPG)], axis=0)


def _nsa_kernel(slopes_ref, q_ref, kc_ref, vct_ref, ks_ref, vst_ref, kw_ref, vwt_ref, kf_ref, gate_ref,
                ovl_ref, o_ref, *, n_sel):
    g = pl.program_id(1)
    qi = pl.program_id(2)
    t0 = qi * NSA_TQ
    cols = NSA_HPG * NSA_TQ

    q = _stack_heads(q_ref[0])
    slope_row = jnp.concatenate(
        [jnp.full((1, NSA_TQ), slopes_ref[SLOPE_PIECES, g * NSA_HPG + h], F32) for h in range(NSA_HPG)],
        axis=1)
    t_row = t0 + (lax.broadcasted_iota(jnp.int32, (1, cols), 1) & (NSA_TQ - 1))

    n_cmp_pad = kc_ref.shape[2]
    cend = (lax.broadcasted_iota(jnp.int32, (n_cmp_pad, 1), 0) * CMP_STRIDE + (CMP_BLOCK - 1))
    rel_c = t_row - cend
    ok_c = rel_c >= 0
    s = _nt_dot(kc_ref[0, 0], q)
    s = s - slope_row * rel_c.astype(F32)
    s = jnp.where(ok_c, s, NEG_INF)
    m = jnp.max(s, axis=0, keepdims=True)
    e = jnp.where(ok_c, jnp.exp2(s - m), 0.0)
    p = e / jnp.maximum(jnp.sum(e, axis=0, keepdims=True), 1e-30)
    o_cmp = jnp.dot(vct_ref[0, 0], p.astype(BF16), preferred_element_type=F32)

    lane = lax.broadcasted_iota(jnp.int32, (1, LANE), 1)
    slope_feat = []
    for h in range(NSA_HPG):
        row = jnp.zeros((1, LANE), F32)
        for part in range(2):
            for piece in range(SLOPE_PIECES):
                row = jnp.where(lane == n_sel + part * SLOPE_PIECES + piece,
                                slopes_ref[piece, g * NSA_HPG + h], row)
        slope_feat.append(jnp.broadcast_to(row, (NSA_TQ, LANE)))
    slope_feat = jnp.concatenate(slope_feat, axis=0)
    q_win = jnp.concatenate([q, slope_feat.astype(BF16)], axis=1)

    def keys_aug(k_ref, start, size):
        return jnp.concatenate([k_ref[0, pl.ds(start, size), :], kf_ref[pl.ds(start, size), :]], axis=1)

    n_pieces = WIN_SIZE // NSA_TQ + 1
    win_scores, win_starts = [], []
    for i in range(n_pieces):
        first = t0 - WIN_SIZE + i * NSA_TQ
        start = pl.multiple_of(jnp.maximum(first, 0), NSA_TQ)
        s_i = _nt_dot(keys_aug(kw_ref, start, NSA_TQ), q_win)
        rel_i = t_row - (first + lax.broadcasted_iota(jnp.int32, (NSA_TQ, 1), 0))
        if i == n_pieces - 1:
            s_i = jnp.where(rel_i >= 0, s_i, NEG_INF)
        elif i == 0:
            s_i = jnp.where(rel_i < jnp.where(first >= 0, WIN_SIZE, -WIN_SPAN), s_i, NEG_INF)
        else:
            s_i = s_i + jnp.where(first >= 0, 0.0, NEG_INF)
        win_scores.append(s_i)
        win_starts.append(start)
    m_w = functools.reduce(jnp.maximum, [jnp.max(s_i, axis=0, keepdims=True) for s_i in win_scores])
    win_probs = [jnp.exp2(s_i - m_w) for s_i in win_scores]
    l_w = sum(jnp.sum(p_i, axis=0, keepdims=True) for p_i in win_probs)
    o_win = sum(jnp.dot(vwt_ref[0, 0, :, pl.ds(start, NSA_TQ)], p_i.astype(BF16), preferred_element_type=F32)
                for start, p_i in zip(win_starts, win_probs)) / l_w

    p_grp = p[:, 0:NSA_TQ]
    for h in range(1, NSA_HPG):
        p_grp = p_grp + p[:, h * NSA_TQ:(h + 1) * NSA_TQ]
    p_hi = p_grp.astype(BF16)
    p_lo = (p_grp - p_hi.astype(F32)).astype(BF16)
    ovl = ovl_ref[...]
    imp_t = (jnp.dot(ovl, p_hi, preferred_element_type=F32)
             + jnp.dot(ovl, p_lo, preferred_element_type=F32))
    blk = lax.broadcasted_iota(jnp.int32, (n_sel, NSA_TQ), 0)
    cur = (t0 + lax.broadcasted_iota(jnp.int32, (n_sel, NSA_TQ), 1)) // SEL_BLOCK
    valid = blk <= cur
    forced = (blk == 0) | (valid & (blk > cur - SEL_FORCED_LOCAL))
    score = jnp.where(valid, imp_t + jnp.where(forced, FORCE_BONUS, 0.0), -1.0)
    rank = jnp.zeros((n_sel, NSA_TQ), jnp.int32)
    for mth in range(n_sel):
        other = score[mth:mth + 1, :]
        tie_first = jnp.where(blk > mth, 1, 0)
        rank = rank + jnp.where(other > score, 1, jnp.where(other == score, tie_first, 0))
    member_t = jnp.where(valid, jnp.where(rank < SEL_TOP_N, 1.0, 0.0), 0.0).astype(BF16)
    qrow =lax.broadcasted_iota(jnp.int32, (cols, NSA_TQ), 0) & (NSA_TQ - 1)
    eye = jnp.where(qrow == lax.broadcasted_iota(jnp.int32, (cols, NSA_TQ), 1), 1.0, 0.0).astype(BF16)
    member_pad = jnp.concatenate([member_t, jnp.ones((LANE - n_sel, NSA_TQ), BF16)], axis=0)
    picked = _nt_dot(eye, member_pad)
    q_sel = jnp.concatenate([q, (jnp.where(picked > 0.5, 0.0, NEG_INF) + slope_feat).astype(BF16)], axis=1)

    def sel_step(j, carry, causal):
        m_i, l_i, acc = carry
        start = pl.multiple_of(j * NSA_TK, NSA_TK)
        s_ = _nt_dot(keys_aug(ks_ref, start, NSA_TK), q_sel)
        if causal:
            pos = start + lax.broadcasted_iota(jnp.int32, (NSA_TK, 1), 0)
            s_ = jnp.where(pos <= t_row, s_, NEG_INF)
        m_new = jnp.maximum(m_i, jnp.max(s_, axis=0, keepdims=True))
        alpha = jnp.exp2(m_i - m_new)
        p_ = jnp.exp2(s_ - m_new)
        l_new = alpha * l_i + jnp.sum(p_, axis=0, keepdims=True)
        acc_new = alpha * acc + jnp.dot(vst_ref[0, 0, :, pl.ds(start, NSA_TK)], p_.astype(BF16),
                                        preferred_element_type=F32)
        return m_new, l_new, acc_new

    init = (jnp.full((1, cols), NEG_INF, F32), jnp.zeros((1, cols), F32),
            jnp.zeros((HEAD_DIM, cols), F32))
    last = (t0 + NSA_TQ - 1) // NSA_TK
    carry = lax.fori_loop(0, last, functools.partial(sel_step, causal=False), init)
    _, l_s, acc_s = sel_step(last, carry, causal=True)
    o_sel = acc_s / l_s

    gates_t = jax.nn.sigmoid(gate_ref[0]).T
    for h in range(NSA_HPG):
        cs = slice(h * NSA_TQ, (h + 1) * NSA_TQ)
        gsel = [gates_t[h * N_GATES + k:h * N_GATES + k + 1, :] for k in range(N_GATES)]
        mix_t = gsel[0] * o_cmp[:, cs] + gsel[1] * o_sel[:, cs] + gsel[2] * o_win[:, cs]
        o_ref[0, :, h * HEAD_DIM:(h + 1) * HEAD_DIM] = mix_t.T


def _key_features(S):
    pos = jnp.arange(S)[:, None]
    lane = jnp.arange(LANE)[None, :]
    n_sel = S // SEL_BLOCK
    feat = jnp.where(lane == pos // SEL_BLOCK, 1, 0)
    feat = jnp.where((lane >= n_sel) & (lane < n_sel + SLOPE_PIECES), pos % SEL_BLOCK, feat)
    feat = jnp.where((lane >= n_sel + SLOPE_PIECES) & (lane < n_sel + 2 * SLOPE_PIECES),
                     (pos // SEL_BLOCK) * SEL_BLOCK, feat)
    return feat.astype(BF16)


def _slope_table(slopes):
    rows, rest = [], slopes
    for _ in range(SLOPE_PIECES):
        piece = rest.astype(BF16).astype(F32)
        rows.append(piece)
        rest = rest - piece
    return jnp.stack(rows + [slopes])


def _nsa_attention(qkv_b, kc, vc_t, vs_t, vw_t, gates, slopes_b, overlap_t, B, S):
    assert S % NSA_TK == 0 and S >= WIN_SPAN
    assert S // SEL_BLOCK <= SEL_BLOCK and S // SEL_BLOCK + 2 * SLOPE_PIECES <= LANE
    b3 = qkv_b.reshape(B, S, COLS_MAIN)
    g3 = gates.reshape(B, S, GATE_PAD)
    n_cmp_pad = kc.shape[2]
    n_sel = S // SEL_BLOCK
    q_w = NSA_HPG * HEAD_DIM
    q0 = COLS_A // q_w
    kv0 = (COLS_A + WIDTH_B) // HEAD_DIM
    per = KV_WIDTH_B // HEAD_DIM
    seq_blk = (1, S, HEAD_DIM)
    seq_t_blk = (1, 1, HEAD_DIM, S)
    grp = lambda b, g, i: (b, g, 0, 0)
    out = pl.pallas_call(
        functools.partial(_nsa_kernel, n_sel=n_sel),
        grid=(B, NSA_KV_GROUPS, S // NSA_TQ),
        in_specs=[
            pl.BlockSpec(memory_space=pltpu.SMEM),
            pl.BlockSpec((1, NSA_TQ, q_w), lambda b, g, i: (b, i, q0 + g)),
            pl.BlockSpec((1, 1, n_cmp_pad, HEAD_DIM), grp),
            pl.BlockSpec((1, 1, HEAD_DIM, n_cmp_pad), grp),
            pl.BlockSpec(seq_blk, lambda b, g, i: (b, 0, kv0 + 2 * per + g)),
            pl.BlockSpec(seq_t_blk, grp),
            pl.BlockSpec(seq_blk, lambda b, g, i: (b, 0, kv0 + 4 * per + g)),
            pl.BlockSpec(seq_t_blk, grp),
            pl.BlockSpec((S, LANE), lambda b, g, i: (0, 0)),
            pl.BlockSpec((1, NSA_TQ, LANE), lambda b, g, i: (b, i, g)),
            pl.BlockSpec((n_sel, n_cmp_pad), lambda b, g, i: (0, 0)),
        ],
        out_specs=pl.BlockSpec((1, NSA_TQ, q_w), lambda b, g, i: (b, i, g)),
        out_shape=jax.ShapeDtypeStruct((B, S, WIDTH_B), F32),
        compiler_params=_cparams(("arbitrary", "arbitrary", "arbitrary")),
        name="nsa_attention",
    )(_slope_table(slopes_b * LOG2_E), b3, kc, vc_t, b3, vs_t, b3, vw_t, _key_features(S), g3, overlap_t)
    return out.reshape(B * S, WIDTH_B)


OUT_TM = 512
OUT_SUB = 128


def _rms(x, g):
    return x * lax.rsqrt(jnp.mean(x * x, axis=-1, keepdims=True) + RMS_EPS) * g


def _outproj_kernel(o1_ref, o2_ref, o3_ref, l1_ref, l2_ref, l3_ref, ob_ref, x_ref, ga_ref, gb_ref,
                    w_ref, g2_ref, h_ref, hn_ref):
    for sub in range(OUT_TM // OUT_SUB):
        rows = slice(sub * OUT_SUB, (sub + 1) * OUT_SUB)
        l1, l2, l3 = l1_ref[rows, :], l2_ref[rows, :], l3_ref[rows, :]
        mx = jnp.maximum(jnp.maximum(l1, l2), l3)
        e1, e2, e3 = jnp.exp(l1 - mx), jnp.exp(l2 - mx), jnp.exp(l3 - mx)
        den = e1 + e2 + e3
        w1, w2, w3 = e1 / den, e2 / den, e3 / den
        parts = []
        for h in range(N_HEADS_A):
            hs = slice(h * HEAD_DIM, (h + 1) * HEAD_DIM)
            c = (h // DIL_HEAD_GROUP) * LANE + h % DIL_HEAD_GROUP
            parts.append(w1[:, c:c + 1] * o1_ref[rows, hs] + w2[:, c:c + 1] * o2_ref[rows, hs]
                         + w3[:, c:c + 1] * o3_ref[rows, hs])
        o_a = jnp.concatenate(parts, axis=-1)
        mixed = jnp.concatenate([_rms(o_a, ga_ref[...]), _rms(ob_ref[rows, :], gb_ref[...])], axis=-1)
        h = x_ref[rows, :] + jnp.dot(mixed.astype(BF16), w_ref[...], preferred_element_type=F32)
        h_ref[rows, :] = h
        hn_ref[rows, :] = _rms(h, g2_ref[...]).astype(BF16)


def _out_projection(o_as, lses, o_b, x2, ga, gb, w_out, g2):
    T = x2.shape[0]
    row = lambda w: pl.BlockSpec((OUT_TM, w), lambda i: (i, 0))
    full = lambda r, c: pl.BlockSpec((r, c), lambda i: (0, 0))
    return pl.pallas_call(
        _outproj_kernel,
        grid=(T // OUT_TM,),
        in_specs=[row(WIDTH_A)] * 3 + [row(DIL_LSE_W)] * 3 + [row(WIDTH_B), row(D_MODEL),
                  full(1, WIDTH_A), full(1, WIDTH_B), full(WIDTH_A + WIDTH_B, D_MODEL), full(1, D_MODEL)],
        out_specs=[row(D_MODEL), row(D_MODEL)],
        out_shape=[jax.ShapeDtypeStruct((T, D_MODEL), F32), jax.ShapeDtypeStruct((T, D_MODEL), BF16)],
        compiler_params=_cparams(("arbitrary",)),
        name="out_projection",
    )(*o_as, *lses, o_b, x2, ga, gb, w_out, g2)


FFN_TM = 512
FFN_TH = 512


def _ffn_kernel(hn_ref, h_ref, wg_ref, wu_ref, wd_ref, gf_ref, o_ref, acc_ref):
    j = pl.program_id(1)

    @pl.when(j == 0)
    def _():
        acc_ref[...] = jnp.zeros(acc_ref.shape, F32)

    hn = hn_ref[...]
    gate = jnp.dot(hn, wg_ref[...], preferred_element_type=F32)
    up = jnp.dot(hn, wu_ref[...], preferred_element_type=F32)
    act = (gate * jax.nn.sigmoid(gate) * up).astype(BF16)
    acc_ref[...] += jnp.dot(act, wd_ref[...], preferred_element_type=F32)

    @pl.when(j == pl.num_programs(1) - 1)
    def _():
        o_ref[...] = _rms(h_ref[...] + acc_ref[...], gf_ref[...])


def _ffn(hn2, h, w_gate, w_up, w_down, gf):
    T = h.shape[0]
    return pl.pallas_call(
        _ffn_kernel,
        grid=(T // FFN_TM, FFN_HIDDEN // FFN_TH),
        in_specs=[
            pl.BlockSpec((FFN_TM, D_MODEL), lambda i, j: (i, 0)),
            pl.BlockSpec((FFN_TM, D_MODEL), lambda i, j: (i, 0)),
            pl.BlockSpec((D_MODEL, FFN_TH), lambda i, j: (0, j)),
            pl.BlockSpec((D_MODEL, FFN_TH), lambda i, j: (0, j)),
            pl.BlockSpec((FFN_TH, D_MODEL), lambda i, j: (j, 0)),
            pl.BlockSpec((1, D_MODEL), lambda i, j: (0, 0)),
        ],
        out_specs=pl.BlockSpec((FFN_TM, D_MODEL), lambda i, j: (i, 0)),
        out_shape=jax.ShapeDtypeStruct((T, D_MODEL), F32),
        scratch_shapes=[pltpu.VMEM((FFN_TM, D_MODEL), F32)],
        compiler_params=_cparams(("arbitrary", "arbitrary")),
        name="swiglu_ffn",
    )(hn2, h, w_gate, w_up, w_down, gf)


def _chunk_view(qkv_b3, col0):
    B, S, _ = qkv_b3.shape
    a = qkv_b3[:, :, col0:col0 + KV_WIDTH_B]
    a = a.reshape(B, S // CMP_STRIDE, CMP_STRIDE, NSA_KV_GROUPS, HEAD_DIM)
    return a.transpose(0, 3, 1, 2, 4).reshape(B, NSA_KV_GROUPS, S // CMP_STRIDE, CHUNK_FLAT)


def _overlap_t(n_sel, n_cmp_pad):
    cstart = jnp.arange(n_cmp_pad)[None, :] * CMP_STRIDE
    sstart = jnp.arange(n_sel)[:, None] * SEL_BLOCK
    return ((cstart < sstart + SEL_BLOCK) & (cstart + CMP_BLOCK > sstart)).astype(BF16)


def kernel(x, norm1_g, w_in, cmp_pe_k, cmp_w1_k, cmp_w2_k, cmp_pe_v, cmp_w1_v, cmp_w2_v, grp_norm_a,
           grp_norm_b, w_out, norm2_g, w_gate, w_up, w_down, final_g):
    B, S, D = x.shape
    assert D == D_MODEL and S % (DIL_BLOCK * max(d for _, d in DIL_CONFIGS)) == 0
    assert (B * S) % IN_TM == 0 and w_in.shape[0] == 1, "single-layer block only"
    T = B * S
    slopes = jnp.exp2(-8.0 * jnp.arange(1, N_HEADS + 1, dtype=F32) / N_HEADS)
    slopes_a, slopes_b = slopes[0::2], slopes[1::2]
    n_main = COLS_A + COLS_B
    per_group = NSA_HPG * N_GATES

    x2 = x.reshape(T, D)
    w_main = w_in[0][:, :n_main].astype(BF16)
    w_g = w_in[0][:, n_main:].reshape(D, NSA_KV_GROUPS, per_group)
    w_g = jnp.pad(w_g, ((0, 0), (0, 0), (0, LANE - per_group))).reshape(D, GATE_PAD).astype(BF16)
    proj, gates = _in_projection(x2, norm1_g[0][None, :], w_main, w_g)

    dil = [_contiguous_attention(proj, slopes_a, B, S, w) if d == 1
           else _strided_attention(proj, slopes_a, B, S, w, d) for w, d in DIL_CONFIGS]

    b3 = proj.reshape(B, S, COLS_MAIN)
    kv_col = COLS_A + WIDTH_B
    pe2 = lambda pe: pe.reshape(2, CHUNK_FLAT)
    kc = _compress(_chunk_view(b3, kv_col), pe2(cmp_pe_k[0]), cmp_w1_k[0].astype(BF16),
                   cmp_w2_k[0].astype(BF16))
    vc = _compress(_chunk_view(b3, kv_col + KV_WIDTH_B), pe2(cmp_pe_v[0]), cmp_w1_v[0].astype(BF16),
                   cmp_w2_v[0].astype(BF16))
    col_vs = kv_col + 3 * KV_WIDTH_B
    col_vw = kv_col + 5 * KV_WIDTH_B
    seq_t = lambda c0: b3[:, :, c0:c0 + KV_WIDTH_B].reshape(B, S, NSA_KV_GROUPS, HEAD_DIM).transpose(0, 2, 3, 1)
    o_b = _nsa_attention(proj, kc, vc.transpose(0, 1, 3, 2), seq_t(col_vs), seq_t(col_vw), gates, slopes_b,
                         _overlap_t(S // SEL_BLOCK, kc.shape[2]), B, S)

    h, hn2 = _out_projection([o for o, _ in dil], [s for _, s in dil], o_b, x2,
                             grp_norm_a[0][None, :], grp_norm_b[0][None, :],
                             w_out[0].astype(BF16), norm2_g[0][None, :])
    out = _ffn(hn2, h, w_gate[0].astype(BF16), w_up[0].astype(BF16), w_down[0].astype(BF16),
               final_g[None, :])
    return out.reshape(B, S, D)
```

```python
import functools
import math

import jax
import jax.numpy as jnp
from jax import lax
from jax.experimental import pallas as pl
from jax.experimental.pallas import tpu as pltpu

F32 = jnp.float32
BF16 = jnp.bfloat16

D_MODEL = 2048
HEAD_DIM = 128
N_HEADS = 16
N_HEADS_A = 8
N_HEADS_B = 8
WIDTH_A = N_HEADS_A * HEAD_DIM
WIDTH_B = N_HEADS_B * HEAD_DIM
DIL_CONFIGS = ((128, 1), (512, 4), (2048, 16))
DIL_BLOCK = 128
NSA_KV_GROUPS = 2
NSA_HPG = N_HEADS_B // NSA_KV_GROUPS
KV_WIDTH_B = NSA_KV_GROUPS * HEAD_DIM
CMP_BLOCK = 32
CMP_STRIDE = 16
CMP_HIDDEN = 2 * HEAD_DIM
SEL_BLOCK = 64
SEL_TOP_N = 16
SEL_FORCED_LOCAL = 2
FORCE_BONUS = 1.0e3
WIN_SIZE = 512
N_GATES = 3
FFN_HIDDEN = 5632
RMS_EPS = 1e-6
NEG_INF = -1e30
LOG2_E = math.log2(math.e)

COLS_A = 3 * WIDTH_A
COLS_B = WIDTH_B + 6 * KV_WIDTH_B
COLS_MAIN = COLS_A + COLS_B
LANE = 128
GATE_PAD = NSA_KV_GROUPS * LANE

VMEM_LIMIT = 56 * 1024 * 1024


def _cparams(sem):
    return pltpu.CompilerParams(dimension_semantics=sem, vmem_limit_bytes=VMEM_LIMIT)


def _nt_dot(a, b):
    return lax.dot_general(a, b, (((1,), (1,)), ((), ())), preferred_element_type=F32)


IN_TM = 1024
IN_TN = 512
N_TILES_A = COLS_A // IN_TN
N_TILES_B = COLS_B // IN_TN
Q_B_TILE0 = N_TILES_A


def _inproj_kernel(x_ref, g_ref, w_ref, wg_ref, o_ref, og_ref, xn_ref, *, scale):
    j = pl.program_id(1)

    @pl.when(j == 0)
    def _():
        x = x_ref[...]
        y = x * lax.rsqrt(jnp.mean(x * x, axis=-1, keepdims=True) + RMS_EPS)
        xn = (y * g_ref[...]).astype(BF16)
        xn_ref[...] = xn
        og_ref[...] = jnp.dot(xn, wg_ref[...], preferred_element_type=F32)

    acc = jnp.dot(xn_ref[...], w_ref[...], preferred_element_type=F32)
    col_scale = jnp.where(j < 2, scale,
                          jnp.where((j >= Q_B_TILE0) & (j < Q_B_TILE0 + 2), scale * LOG2_E, 1.0))
    o_ref[...] = (acc * col_scale.astype(F32)).astype(BF16)


def _in_projection(x2, g, w_main, w_gate):
    T = x2.shape[0]
    grid = (T // IN_TM, N_TILES_A + N_TILES_B)
    return pl.pallas_call(
        functools.partial(_inproj_kernel, scale=1.0 / math.sqrt(HEAD_DIM)),
        grid=grid,
        in_specs=[
            pl.BlockSpec((IN_TM, D_MODEL), lambda i, j: (i, 0)),
            pl.BlockSpec((1, D_MODEL), lambda i, j: (0, 0)),
            pl.BlockSpec((D_MODEL, IN_TN), lambda i, j: (0, j)),
            pl.BlockSpec((D_MODEL, GATE_PAD), lambda i, j: (0, 0)),
        ],
        out_specs=[
            pl.BlockSpec((IN_TM, IN_TN), lambda i, j: (i, j)),
            pl.BlockSpec((IN_TM, GATE_PAD), lambda i, j: (i, 0)),
        ],
        out_shape=[
            jax.ShapeDtypeStruct((T, COLS_MAIN), BF16),
            jax.ShapeDtypeStruct((T, GATE_PAD), F32),
        ],
        scratch_shapes=[pltpu.VMEM((IN_TM, D_MODEL), BF16)],
        compiler_params=_cparams(("arbitrary", "arbitrary")),
        name="in_projection",
    )(x2, g, w_main, w_gate)


DIL_HEAD_GROUP = 4
DIL_COLS = DIL_HEAD_GROUP * HEAD_DIM
DIL_LSE_W = (N_HEADS_A // DIL_HEAD_GROUP) * LANE
DIL_PERM = 256
D1_BLOCKS = 2


def _dilated_heads(q, k_prev, k_cur, v_prev, v_cur, slopes, has_prev, dilation, span):
    key = lax.broadcasted_iota(jnp.int32, (2 * DIL_BLOCK, DIL_BLOCK), 0)
    qry = lax.broadcasted_iota(jnp.int32, (2 * DIL_BLOCK, DIL_BLOCK), 1)
    rel = qry + DIL_BLOCK - key
    in_prev = jnp.where(rel <= span, jnp.where(has_prev, 0.0, NEG_INF), NEG_INF)
    mask_bias = jnp.where(key < DIL_BLOCK, in_prev, jnp.where(rel >= 0, 0.0, NEG_INF))
    dist = (rel * dilation).astype(F32)
    heads = [slice(h * HEAD_DIM, (h + 1) * HEAD_DIM) for h in range(len(slopes))]
    scores = [_nt_dot(jnp.concatenate([k_prev[:, hs], k_cur[:, hs]], axis=0), q[:, hs]) for hs in heads]
    scores = [s - slope * dist + mask_bias for slope, s in zip(slopes, scores)]
    maxes = [jnp.max(s, axis=0, keepdims=True) for s in scores]
    probs = [jnp.exp(s - m) for s, m in zip(scores, maxes)]
    sums = [jnp.sum(p, axis=0, keepdims=True) for p in probs]
    outs = [lax.dot_general((p * (1.0 / l)).astype(BF16),
                            jnp.concatenate([v_prev[:, hs], v_cur[:, hs]], axis=0),
                            (((0,), (0,)), ((), ())), preferred_element_type=F32)
            for hs, p, l in zip(heads, probs, sums)]
    return (jnp.concatenate(outs, axis=1).astype(BF16),
            [m + jnp.log(l) for m, l in zip(maxes, sums)])


def _lse_tile(rows):
    pad = jnp.zeros((LANE - len(rows), DIL_BLOCK), F32)
    return jnp.concatenate(list(rows) + [pad], axis=0).T


def _dilated_kernel(slopes_ref, q_ref, kc_ref, kp_ref, vc_ref, vp_ref, o_ref, lse_ref, *,
                    dilation, span):
    slopes = [slopes_ref[h] for h in range(N_HEADS_A)]
    for i in range(D1_BLOCKS):
        rows = slice(i * DIL_BLOCK, (i + 1) * DIL_BLOCK)
        before = slice((i - 1) * DIL_BLOCK, i * DIL_BLOCK)
        k_prev, v_prev = (kp_ref[0], vp_ref[0]) if i == 0 else (kc_ref[0, before, :], vc_ref[0, before, :])
        has_prev = (pl.program_id(2) > 0) if i == 0 else True
        o, lse = _dilated_heads(q_ref[0, rows, :], k_prev, kc_ref[0, rows, :], v_prev, vc_ref[0, rows, :],
                                slopes, has_prev, dilation, span)
        o_ref[0, rows, :] = o
        for grp in range(N_HEADS_A // DIL_HEAD_GROUP):
            lse_ref[0, rows, grp * LANE:(grp + 1) * LANE] = _lse_tile(
                lse[grp * DIL_HEAD_GROUP:(grp + 1) * DIL_HEAD_GROUP])


def _split_bf16(x, pieces):
    out, rest = [], x
    for _ in range(pieces):
        part = rest.astype(BF16)
        out.append(part)
        rest = rest - part.astype(F32)
    return out


def _strided_kernel(slopes_ref, perm_ref, perm_t_ref, q_ref, k_ref, v_ref, o_ref, lse_ref,
                    qp_ref, kp_ref, vp_ref, op_ref, lp_ref, *, dilation, span):
    grp = pl.program_id(1)
    n = pl.program_id(2)
    n_sub = q_ref.shape[1] // DIL_PERM
    width = DIL_PERM // dilation
    cur = n & 1
    prev = 1 - cur
    perm = perm_ref[...]

    def deinterleave(x):
        y = jnp.dot(perm, x, preferred_element_type=F32).astype(BF16)
        return y.reshape(dilation, width, x.shape[1])

    for sub in range(n_sub):
        rows = slice(sub * DIL_PERM, (sub + 1) * DIL_PERM)
        qp_ref[:, sub] = deinterleave(q_ref[0, rows, :])
        kp_ref[cur, :, sub] = deinterleave(k_ref[0, rows, :])
        vp_ref[cur, :, sub] = deinterleave(v_ref[0, rows, :])

    @pl.when(n == 0)
    def _():
        kp_ref[prev] = jnp.zeros(kp_ref.shape[1:], BF16)
        vp_ref[prev] = jnp.zeros(vp_ref.shape[1:], BF16)

    slopes = [slopes_ref[grp * DIL_HEAD_GROUP + i] for i in range(DIL_HEAD_GROUP)]
    blk_rows = lambda a: a.reshape(DIL_BLOCK, a.shape[-1])

    def residues(i, carry):
        for r in (2 * i, 2 * i + 1):
            o, lse = _dilated_heads(blk_rows(qp_ref[r]), blk_rows(kp_ref[prev, r]), blk_rows(kp_ref[cur, r]),
                                    blk_rows(vp_ref[prev, r]), blk_rows(vp_ref[cur, r]), slopes, n > 0,
                                    dilation, span)
            op_ref[r] = o.reshape(n_sub, width, DIL_COLS)
            lp_ref[r] = _lse_tile(lse).reshape(n_sub, width, LANE)
        return carry

    lax.fori_loop(0, dilation // 2, residues, 0)

    perm_t = perm_t_ref[...]
    for sub in range(n_sub):
        rows = slice(sub * DIL_PERM, (sub + 1) * DIL_PERM)
        o_sub = op_ref[:, sub].reshape(DIL_PERM, DIL_COLS)
        o_ref[0, rows, :] = jnp.dot(perm_t, o_sub, preferred_element_type=F32).astype(BF16)
        l_sub = lp_ref[:, sub].reshape(DIL_PERM, LANE)
        lse_ref[0, rows, :] = sum(jnp.dot(perm_t, part, preferred_element_type=F32)
                                  for part in _split_bf16(l_sub, 3))


def _contiguous_attention(qkv_a, slopes_a, B, S, window):
    rows = D1_BLOCKS * DIL_BLOCK
    assert S % rows == 0
    a3 = qkv_a.reshape(B, S, COLS_MAIN)
    blk = (1, rows, WIDTH_A)
    one = (1, DIL_BLOCK, WIDTH_A)
    prev = lambda n: jnp.maximum(n * D1_BLOCKS - 1, 0)
    o, lse = pl.pallas_call(
        functools.partial(_dilated_kernel, dilation=1, span=window),
        grid=(B, 1, S // rows),
        in_specs=[
            pl.BlockSpec(memory_space=pltpu.SMEM),
            pl.BlockSpec(blk, lambda b, r, n: (b, n, 0)),
            pl.BlockSpec(blk, lambda b, r, n: (b, n, 1)),
            pl.BlockSpec(one, lambda b, r, n: (b, prev(n), 1)),
            pl.BlockSpec(blk, lambda b, r, n: (b, n, 2)),
            pl.BlockSpec(one, lambda b, r, n: (b, prev(n), 2)),
        ],
        out_specs=[
            pl.BlockSpec(blk, lambda b, r, n: (b, n, 0)),
            pl.BlockSpec((1, rows, DIL_LSE_W), lambda b, r, n: (b, n, 0)),
        ],
        out_shape=[
            jax.ShapeDtypeStruct((B, S, WIDTH_A), BF16),
            jax.ShapeDtypeStruct((B, S, DIL_LSE_W), F32),
        ],
        compiler_params=_cparams(("arbitrary", "arbitrary", "arbitrary")),
        name="dilated_attention_d1",
    )(slopes_a, a3, a3, a3, a3, a3)
    return o.reshape(B * S, WIDTH_A), lse.reshape(B * S, DIL_LSE_W)


def _permutation(dilation):
    width = DIL_PERM // dilation
    out_row = jnp.arange(DIL_PERM)
    src = (out_row % width) * dilation + out_row // width
    return (src[:, None] == jnp.arange(DIL_PERM)[None, :]).astype(BF16)


def _strided_attention(qkv_a, slopes_a, B, S, window, dilation):
    rows = DIL_BLOCK * dilation
    assert S % rows == 0 and rows % DIL_PERM == 0 and DIL_PERM % dilation == 0
    assert (DIL_PERM // dilation) % 16 == 0
    assert dilation % 2 == 0
    n_sub, width = rows // DIL_PERM, DIL_PERM // dilation
    n_grp = N_HEADS_A // DIL_HEAD_GROUP
    per = WIDTH_A // DIL_COLS
    a3 = qkv_a.reshape(B, S, COLS_MAIN)
    perm = _permutation(dilation)
    blk = (1, rows, DIL_COLS)
    whole = lambda b, g, n: (0, 0)
    packed = lambda dt: pltpu.VMEM((dilation, n_sub, width, DIL_COLS), dt)
    o, lse = pl.pallas_call(
        functools.partial(_strided_kernel, dilation=dilation, span=window // dilation),
        grid=(B, n_grp, S // rows),
        in_specs=[
            pl.BlockSpec(memory_space=pltpu.SMEM),
            pl.BlockSpec((DIL_PERM, DIL_PERM), whole),
            pl.BlockSpec((DIL_PERM, DIL_PERM), whole),
            pl.BlockSpec(blk, lambda b, g, n: (b, n, g)),
            pl.BlockSpec(blk, lambda b, g, n: (b, n, per + g)),
            pl.BlockSpec(blk, lambda b, g, n: (b, n, 2 * per + g)),
        ],
        out_specs=[
            pl.BlockSpec(blk, lambda b, g, n: (b, n, g)),
            pl.BlockSpec((1, rows, LANE), lambda b, g, n: (b, n, g)),
        ],
        out_shape=[
            jax.ShapeDtypeStruct((B, S, WIDTH_A), BF16),
            jax.ShapeDtypeStruct((B, S, DIL_LSE_W), F32),
        ],
        scratch_shapes=[
            packed(BF16),
            pltpu.VMEM((2, dilation, n_sub, width, DIL_COLS), BF16),
            pltpu.VMEM((2, dilation, n_sub, width, DIL_COLS), BF16),
            packed(BF16),
            pltpu.VMEM((dilation, n_sub, width, LANE), F32),
        ],
        compiler_params=_cparams(("arbitrary", "arbitrary", "arbitrary")),
        name=f"dilated_attention_d{dilation}",
    )(slopes_a, perm, perm.T, a3, a3, a3)
    return o.reshape(B * S, WIDTH_A), lse.reshape(B * S, DIL_LSE_W)


CHUNK_FLAT = CMP_STRIDE * HEAD_DIM


def _compress_kernel(x_ref, pe_ref, w1_ref, w2_ref, o_ref):
    x = x_ref[0, 0].astype(F32)
    first = jnp.dot((x + pe_ref[0:1, :]).astype(BF16), w1_ref[0:CHUNK_FLAT, :],
                    preferred_element_type=F32)
    second = jnp.dot((x + pe_ref[1:2, :]).astype(BF16), w1_ref[CHUNK_FLAT:2 * CHUNK_FLAT, :],
                     preferred_element_type=F32)
    nc = x.shape[0]
    hidden = first + pltpu.roll(second, shift=nc - 1, axis=0)
    act = hidden * jax.nn.sigmoid(hidden)
    o_ref[0, 0] = jnp.dot(act.astype(BF16), w2_ref[...], preferred_element_type=F32).astype(BF16)


def _compress(chunks, pe2, w1, w2):
    B, G, nc, _ = chunks.shape
    return pl.pallas_call(
        _compress_kernel,
        grid=(B, G),
        in_specs=[
            pl.BlockSpec((1, 1, nc, CHUNK_FLAT), lambda b, g: (b, g, 0, 0)),
            pl.BlockSpec((2, CHUNK_FLAT), lambda b, g: (0, 0)),
            pl.BlockSpec((2 * CHUNK_FLAT, CMP_HIDDEN), lambda b, g: (0, 0)),
            pl.BlockSpec((CMP_HIDDEN, HEAD_DIM), lambda b, g: (0, 0)),
        ],
        out_specs=pl.BlockSpec((1, 1, nc, HEAD_DIM), lambda b, g: (b, g, 0, 0)),
        out_shape=jax.ShapeDtypeStruct((B, G, nc, HEAD_DIM), BF16),
        compiler_params=_cparams(("arbitrary", "arbitrary")),
        name="nsa_compress",
    )(chunks, pe2, w1, w2)


NSA_TQ = 256
NSA_TK = 1024
WIN_SPAN = WIN_SIZE + NSA_TQ
SLOPE_PIECES = 3


def _stack_heads(a):
    return jnp.concatenate([a[:, h * HEAD_DIM:(h + 1) * HEAD_DIM] for h in range(NSA_HPG)], axis=0)


def _nsa_kernel(slopes_ref, q_ref, kc_ref, vct_ref, ks_ref, vst_ref, kw_ref, vwt_ref, kf_ref, gate_ref,
                ovl_ref, o_ref, *, n_sel):
    g = pl.program_id(1)
    qi = pl.program_id(2)
    t0 = qi * NSA_TQ
    cols = NSA_HPG * NSA_TQ

    q = _stack_heads(q_ref[0])
    slope_row = jnp.concatenate(
        [jnp.full((1, NSA_TQ), slopes_ref[SLOPE_PIECES, g * NSA_HPG + h], F32) for h in range(NSA_HPG)],
        axis=1)
    t_row = t0 + (lax.broadcasted_iota(jnp.int32, (1, cols), 1) & (NSA_TQ - 1))

    n_cmp_pad = kc_ref.shape[2]
    cend = (lax.broadcasted_iota(jnp.int32, (n_cmp_pad, 1), 0) * CMP_STRIDE + (CMP_BLOCK - 1))
    rel_c = t_row - cend
    ok_c = rel_c >= 0
    s = _nt_dot(kc_ref[0, 0], q)
    s = s - slope_row * rel_c.astype(F32)
    s = jnp.where(ok_c, s, NEG_INF)
    m = jnp.max(s, axis=0, keepdims=True)
    e = jnp.where(ok_c, jnp.exp2(s - m), 0.0)
    p = e / jnp.maximum(jnp.sum(e, axis=0, keepdims=True), 1e-30)
    o_cmp = jnp.dot(vct_ref[0, 0], p.astype(BF16), preferred_element_type=F32)

    lane = lax.broadcasted_iota(jnp.int32, (1, LANE), 1)
    slope_feat = []
    for h in range(NSA_HPG):
        row = jnp.zeros((1, LANE), F32)
        for part in range(2):
            for piece in range(SLOPE_PIECES):
                row = jnp.where(lane == n_sel + part * SLOPE_PIECES + piece,
                                slopes_ref[piece, g * NSA_HPG + h], row)
        slope_feat.append(jnp.broadcast_to(row, (NSA_TQ, LANE)))
    slope_feat = jnp.concatenate(slope_feat, axis=0)
    q_win = jnp.concatenate([q, slope_feat.astype(BF16)], axis=1)

    def keys_aug(k_ref, start, size):
        return jnp.concatenate([k_ref[0, pl.ds(start, size), :], kf_ref[pl.ds(start, size), :]], axis=1)

    n_pieces = WIN_SIZE // NSA_TQ + 1
    win_scores, win_starts = [], []
    for i in range(n_pieces):
        first = t0 - WIN_SIZE + i * NSA_TQ
        start = pl.multiple_of(jnp.maximum(first, 0), NSA_TQ)
        s_i = _nt_dot(keys_aug(kw_ref, start, NSA_TQ), q_win)
        rel_i = t_row - (first + lax.broadcasted_iota(jnp.int32, (NSA_TQ, 1), 0))
        if i == n_pieces - 1:
            s_i = jnp.where(rel_i >= 0, s_i, NEG_INF)
        elif i == 0:
            s_i = jnp.where(rel_i < jnp.where(first >= 0, WIN_SIZE, -WIN_SPAN), s_i, NEG_INF)
        else:
            s_i = s_i + jnp.where(first >= 0, 0.0, NEG_INF)
        win_scores.append(s_i)
        win_starts.append(start)
    m_w = functools.reduce(jnp.maximum, [jnp.max(s_i, axis=0, keepdims=True) for s_i in win_scores])
    win_probs = [jnp.exp2(s_i - m_w) for s_i in win_scores]
    l_w = sum(jnp.sum(p_i, axis=0, keepdims=True) for p_i in win_probs)
    o_win = sum(jnp.dot(vwt_ref[0, 0, :, pl.ds(start, NSA_TQ)], p_i.astype(BF16), preferred_element_type=F32)
                for start, p_i in zip(win_starts, win_probs)) / l_w

    p_grp = p[:, 0:NSA_TQ]
    for h in range(1, NSA_HPG):
        p_grp = p_grp + p[:, h * NSA_TQ:(h + 1) * NSA_TQ]
    p_hi = p_grp.astype(BF16)
    p_lo = (p_grp - p_hi.astype(F32)).astype(BF16)
    ovl = ovl_ref[...]
    imp_t = (jnp.dot(ovl, p_hi, preferred_element_type=F32)
             + jnp.dot(ovl, p_lo, preferred_element_type=F32))
    blk = lax.broadcasted_iota(jnp.int32, (n_sel, NSA_TQ), 0)
    cur = (t0 + lax.broadcasted_iota(jnp.int32, (n_sel, NSA_TQ), 1)) // SEL_BLOCK
    valid = blk <= cur
    forced = (blk == 0) | (valid & (blk > cur - SEL_FORCED_LOCAL))
    score = jnp.where(valid, imp_t + jnp.where(forced, FORCE_BONUS, 0.0), -1.0)
    rank = jnp.zeros((n_sel, NSA_TQ), jnp.int32)
    for mth in range(n_sel):
        other = score[mth:mth + 1, :]
        tie_first = jnp.where(blk > mth, 1, 0)
        rank = rank + jnp.where(other > score, 1, jnp.where(other == score, tie_first, 0))
    member_t = jnp.where(valid, jnp.where(rank < SEL_TOP_N, 1.0, 0.0), 0.0).astype(BF16)
    qrow =lax.broadcasted_iota(jnp.int32, (cols, NSA_TQ), 0) & (NSA_TQ - 1)
    eye = jnp.where(qrow == lax.broadcasted_iota(jnp.int32, (cols, NSA_TQ), 1), 1.0, 0.0).astype(BF16)
    member_pad = jnp.concatenate([member_t, jnp.ones((LANE - n_sel, NSA_TQ), BF16)], axis=0)
    picked = _nt_dot(eye, member_pad)
    q_sel = jnp.concatenate([q, (jnp.where(picked > 0.5, 0.0, NEG_INF) + slope_feat).astype(BF16)], axis=1)

    def sel_step(j, carry, causal):
        m_i, l_i, acc = carry
        start = pl.multiple_of(j * NSA_TK, NSA_TK)
        s_ = _nt_dot(keys_aug(ks_ref, start, NSA_TK), q_sel)
        if causal:
            pos = start + lax.broadcasted_iota(jnp.int32, (NSA_TK, 1), 0)
            s_ = jnp.where(pos <= t_row, s_, NEG_INF)
        m_new = jnp.maximum(m_i, jnp.max(s_, axis=0, keepdims=True))
        alpha = jnp.exp2(m_i - m_new)
        p_ = jnp.exp2(s_ - m_new)
        l_new = alpha * l_i + jnp.sum(p_, axis=0, keepdims=True)
        acc_new = alpha * acc + jnp.dot(vst_ref[0, 0, :, pl.ds(start, NSA_TK)], p_.astype(BF16),
                                        preferred_element_type=F32)
        return m_new, l_new, acc_new

    init = (jnp.full((1, cols), NEG_INF, F32), jnp.zeros((1, cols), F32),
            jnp.zeros((HEAD_DIM, cols), F32))
    last = (t0 + NSA_TQ - 1) // NSA_TK
    carry = lax.fori_loop(0, last, functools.partial(sel_step, causal=False), init)
    _, l_s, acc_s = sel_step(last, carry, causal=True)
    o_sel = acc_s / l_s

    gates_t = jax.nn.sigmoid(gate_ref[0]).T
    for h in range(NSA_HPG):
        cs = slice(h * NSA_TQ, (h + 1) * NSA_TQ)
        gsel = [gates_t[h * N_GATES + k:h * N_GATES + k + 1, :] for k in range(N_GATES)]
        mix_t = gsel[0] * o_cmp[:, cs] + gsel[1] * o_sel[:, cs] + gsel[2] * o_win[:, cs]
        o_ref[0, :, h * HEAD_DIM:(h + 1) * HEAD_DIM] = mix_t.T


def _key_features(S):
    pos = jnp.arange(S)[:, None]
    lane = jnp.arange(LANE)[None, :]
    n_sel = S // SEL_BLOCK
    feat = jnp.where(lane == pos // SEL_BLOCK, 1, 0)
    feat = jnp.where((lane >= n_sel) & (lane < n_sel + SLOPE_PIECES), pos % SEL_BLOCK, feat)
    feat = jnp.where((lane >= n_sel + SLOPE_PIECES) & (lane < n_sel + 2 * SLOPE_PIECES),
                     (pos // SEL_BLOCK) * SEL_BLOCK, feat)
    return feat.astype(BF16)


def _slope_table(slopes):
    rows, rest = [], slopes
    for _ in range(SLOPE_PIECES):
        piece = rest.astype(BF16).astype(F32)
        rows.append(piece)
        rest = rest - piece
    return jnp.stack(rows + [slopes])


def _nsa_attention(qkv_b, kc, vc_t, vs_t, vw_t, gates, slopes_b, overlap_t, B, S):
    assert S % NSA_TK == 0 and S >= WIN_SPAN
    assert S // SEL_BLOCK <= SEL_BLOCK and S // SEL_BLOCK + 2 * SLOPE_PIECES <= LANE
    b3 = qkv_b.reshape(B, S, COLS_MAIN)
    g3 = gates.reshape(B, S, GATE_PAD)
    n_cmp_pad = kc.shape[2]
    n_sel = S // SEL_BLOCK
    q_w = NSA_HPG * HEAD_DIM
    q0 = COLS_A // q_w
    kv0 = (COLS_A + WIDTH_B) // HEAD_DIM
    per = KV_WIDTH_B // HEAD_DIM
    seq_blk = (1, S, HEAD_DIM)
    seq_t_blk = (1, 1, HEAD_DIM, S)
    grp = lambda b, g, i: (b, g, 0, 0)
    out = pl.pallas_call(
        functools.partial(_nsa_kernel, n_sel=n_sel),
        grid=(B, NSA_KV_GROUPS, S // NSA_TQ),
        in_specs=[
            pl.BlockSpec(memory_space=pltpu.SMEM),
            pl.BlockSpec((1, NSA_TQ, q_w), lambda b, g, i: (b, i, q0 + g)),
            pl.BlockSpec((1, 1, n_cmp_pad, HEAD_DIM), grp),
            pl.BlockSpec((1, 1, HEAD_DIM, n_cmp_pad), grp),
            pl.BlockSpec(seq_blk, lambda b, g, i: (b, 0, kv0 + 2 * per + g)),
            pl.BlockSpec(seq_t_blk, grp),
            pl.BlockSpec(seq_blk, lambda b, g, i: (b, 0, kv0 + 4 * per + g)),
            pl.BlockSpec(seq_t_blk, grp),
            pl.BlockSpec((S, LANE), lambda b, g, i: (0, 0)),
            pl.BlockSpec((1, NSA_TQ, LANE), lambda b, g, i: (b, i, g)),
            pl.BlockSpec((n_sel, n_cmp_pad), lambda b, g, i: (0, 0)),
        ],
        out_specs=pl.BlockSpec((1, NSA_TQ, q_w), lambda b, g, i: (b, i, g)),
        out_shape=jax.ShapeDtypeStruct((B, S, WIDTH_B), F32),
        compiler_params=_cparams(("arbitrary", "arbitrary", "arbitrary")),
        name="nsa_attention",
    )(_slope_table(slopes_b * LOG2_E), b3, kc, vc_t, b3, vs_t, b3, vw_t, _key_features(S), g3, overlap_t)
    return out.reshape(B * S, WIDTH_B)


OUT_TM = 512
OUT_SUB = 128


def _rms(x, g):
    return x * lax.rsqrt(jnp.mean(x * x, axis=-1, keepdims=True) + RMS_EPS) * g


def _outproj_kernel(o1_ref, o2_ref, o3_ref, l1_ref, l2_ref, l3_ref, ob_ref, x_ref, ga_ref, gb_ref,
                    w_ref, g2_ref, h_ref, hn_ref):
    for sub in range(OUT_TM // OUT_SUB):
        rows = slice(sub * OUT_SUB, (sub + 1) * OUT_SUB)
        l1, l2, l3 = l1_ref[rows, :], l2_ref[rows, :], l3_ref[rows, :]
        mx = jnp.maximum(jnp.maximum(l1, l2), l3)
        e1, e2, e3 = jnp.exp(l1 - mx), jnp.exp(l2 - mx), jnp.exp(l3 - mx)
        den = e1 + e2 + e3
        w1, w2, w3 = e1 / den, e2 / den, e3 / den
        parts = []
        for h in range(N_HEADS_A):
            hs = slice(h * HEAD_DIM, (h + 1) * HEAD_DIM)
            c = (h // DIL_HEAD_GROUP) * LANE + h % DIL_HEAD_GROUP
            parts.append(w1[:, c:c + 1] * o1_ref[rows, hs] + w2[:, c:c + 1] * o2_ref[rows, hs]
                         + w3[:, c:c + 1] * o3_ref[rows, hs])
        o_a = jnp.concatenate(parts, axis=-1)
        mixed = jnp.concatenate([_rms(o_a, ga_ref[...]), _rms(ob_ref[rows, :], gb_ref[...])], axis=-1)
        h = x_ref[rows, :] + jnp.dot(mixed.astype(BF16), w_ref[...], preferred_element_type=F32)
        h_ref[rows, :] = h
        hn_ref[rows, :] = _rms(h, g2_ref[...]).astype(BF16)


def _out_projection(o_as, lses, o_b, x2, ga, gb, w_out, g2):
    T = x2.shape[0]
    row = lambda w: pl.BlockSpec((OUT_TM, w), lambda i: (i, 0))
    full = lambda r, c: pl.BlockSpec((r, c), lambda i: (0, 0))
    return pl.pallas_call(
        _outproj_kernel,
        grid=(T // OUT_TM,),
        in_specs=[row(WIDTH_A)] * 3 + [row(DIL_LSE_W)] * 3 + [row(WIDTH_B), row(D_MODEL),
                  full(1, WIDTH_A), full(1, WIDTH_B), full(WIDTH_A + WIDTH_B, D_MODEL), full(1, D_MODEL)],
        out_specs=[row(D_MODEL), row(D_MODEL)],
        out_shape=[jax.ShapeDtypeStruct((T, D_MODEL), F32), jax.ShapeDtypeStruct((T, D_MODEL), BF16)],
        compiler_params=_cparams(("arbitrary",)),
        name="out_projection",
    )(*o_as, *lses, o_b, x2, ga, gb, w_out, g2)


FFN_TM = 512
FFN_TH = 512


def _ffn_kernel(hn_ref, h_ref, wg_ref, wu_ref, wd_ref, gf_ref, o_ref, acc_ref):
    j = pl.program_id(1)

    @pl.when(j == 0)
    def _():
        acc_ref[...] = jnp.zeros(acc_ref.shape, F32)

    hn = hn_ref[...]
    gate = jnp.dot(hn, wg_ref[...], preferred_element_type=F32)
    up = jnp.dot(hn, wu_ref[...], preferred_element_type=F32)
    act = (gate * jax.nn.sigmoid(gate) * up).astype(BF16)
    acc_ref[...] += jnp.dot(act, wd_ref[...], preferred_element_type=F32)

    @pl.when(j == pl.num_programs(1) - 1)
    def _():
        o_ref[...] = _rms(h_ref[...] + acc_ref[...], gf_ref[...])


def _ffn(hn2, h, w_gate, w_up, w_down, gf):
    T = h.shape[0]
    return pl.pallas_call(
        _ffn_kernel,
        grid=(T // FFN_TM, FFN_HIDDEN // FFN_TH),
        in_specs=[
            pl.BlockSpec((FFN_TM, D_MODEL), lambda i, j: (i, 0)),
            pl.BlockSpec((FFN_TM, D_MODEL), lambda i, j: (i, 0)),
            pl.BlockSpec((D_MODEL, FFN_TH), lambda i, j: (0, j)),
            pl.BlockSpec((D_MODEL, FFN_TH), lambda i, j: (0, j)),
            pl.BlockSpec((FFN_TH, D_MODEL), lambda i, j: (j, 0)),
            pl.BlockSpec((1, D_MODEL), lambda i, j: (0, 0)),
        ],
        out_specs=pl.BlockSpec((FFN_TM, D_MODEL), lambda i, j: (i, 0)),
        out_shape=jax.ShapeDtypeStruct((T, D_MODEL), F32),
        scratch_shapes=[pltpu.VMEM((FFN_TM, D_MODEL), F32)],
        compiler_params=_cparams(("arbitrary", "arbitrary")),
        name="swiglu_ffn",
    )(hn2, h, w_gate, w_up, w_down, gf)


def _chunk_view(qkv_b3, col0):
    B, S, _ = qkv_b3.shape
    a = qkv_b3[:, :, col0:col0 + KV_WIDTH_B]
    a = a.reshape(B, S // CMP_STRIDE, CMP_STRIDE, NSA_KV_GROUPS, HEAD_DIM)
    return a.transpose(0, 3, 1, 2, 4).reshape(B, NSA_KV_GROUPS, S // CMP_STRIDE, CHUNK_FLAT)


def _overlap_t(n_sel, n_cmp_pad):
    cstart = jnp.arange(n_cmp_pad)[None, :] * CMP_STRIDE
    sstart = jnp.arange(n_sel)[:, None] * SEL_BLOCK
    return ((cstart < sstart + SEL_BLOCK) & (cstart + CMP_BLOCK > sstart)).astype(BF16)


def kernel(x, norm1_g, w_in, cmp_pe_k, cmp_w1_k, cmp_w2_k, cmp_pe_v, cmp_w1_v, cmp_w2_v, grp_norm_a,
           grp_norm_b, w_out, norm2_g, w_gate, w_up, w_down, final_g):
    B, S, D = x.shape
    assert D == D_MODEL and S % (DIL_BLOCK * max(d for _, d in DIL_CONFIGS)) == 0
    assert (B * S) % IN_TM == 0 and w_in.shape[0] == 1, "single-layer block only"
    T = B * S
    slopes = jnp.exp2(-8.0 * jnp.arange(1, N_HEADS + 1, dtype=F32) / N_HEADS)
    slopes_a, slopes_b = slopes[0::2], slopes[1::2]
    n_main = COLS_A + COLS_B
    per_group = NSA_HPG * N_GATES

    x2 = x.reshape(T, D)
    w_main = w_in[0][:, :n_main].astype(BF16)
    w_g = w_in[0][:, n_main:].reshape(D, NSA_KV_GROUPS, per_group)
    w_g = jnp.pad(w_g, ((0, 0), (0, 0), (0, LANE - per_group))).reshape(D, GATE_PAD).astype(BF16)
    proj, gates = _in_projection(x2, norm1_g[0][None, :], w_main, w_g)

    dil = [_contiguous_attention(proj, slopes_a, B, S, w) if d == 1
           else _strided_attention(proj, slopes_a, B, S, w, d) for w, d in DIL_CONFIGS]

    b3 = proj.reshape(B, S, COLS_MAIN)
    kv_col = COLS_A + WIDTH_B
    pe2 = lambda pe: pe.reshape(2, CHUNK_FLAT)
    kc = _compress(_chunk_view(b3, kv_col), pe2(cmp_pe_k[0]), cmp_w1_k[0].astype(BF16),
                   cmp_w2_k[0].astype(BF16))
    vc = _compress(_chunk_view(b3, kv_col + KV_WIDTH_B), pe2(cmp_pe_v[0]), cmp_w1_v[0].astype(BF16),
                   cmp_w2_v[0].astype(BF16))
    col_vs = kv_col + 3 * KV_WIDTH_B
    col_vw = kv_col + 5 * KV_WIDTH_B
    seq_t = lambda c0: b3[:, :, c0:c0 + KV_WIDTH_B].reshape(B, S, NSA_KV_GROUPS, HEAD_DIM).transpose(0, 2, 3, 1)
    o_b = _nsa_attention(proj, kc, vc.transpose(0, 1, 3, 2), seq_t(col_vs), seq_t(col_vw), gates, slopes_b,
                         _overlap_t(S // SEL_BLOCK, kc.shape[2]), B, S)

    h, hn2 = _out_projection([o for o, _ in dil], [s for _, s in dil], o_b, x2,
                             grp_norm_a[0][None, :], grp_norm_b[0][None, :],
                             w_out[0].astype(BF16), norm2_g[0][None, :])
    out = _ffn(hn2, h, w_gate[0].astype(BF16), w_up[0].astype(BF16), w_down[0].astype(BF16),
               final_g[None, :])
    return out.reshape(B, S, D)
```
